```python
import math
import jax, jax.numpy as jnp
from jax import lax
import numpy as np

D_MODEL = 1024
BATCH = 8
SEQ = 4096
DEPTH = 4

D_FF = 2816
NORM_EPS = 1e-6
N_BRANCH = 4

GDN_HEADS = 4
GDN_DK = 64
GDN_DV = 64
GDN_CONV = 4
GDN_CHUNK = 64

S5_GROUPS = 16
S5_GROUP_WIDTH = 16
S5_STATE = 64
S5_WIDTH = S5_GROUPS * S5_GROUP_WIDTH

LRU_WIDTH = 256
LRU_BLOCKS = 4
LRU_BLOCK_WIDTH = LRU_WIDTH // LRU_BLOCKS
LRU_CONV = 4
LRU_C = 8.0

RET_HEADS = 4
RET_DK = 64
RET_DV = 64
RET_CHUNK = 128
ROPE_BASE = 10000.0

BRANCH_WIDTH = 256

IN_SPLITS = (
    GDN_HEADS * (2 * GDN_DK + GDN_DV),
    GDN_HEADS * GDN_DV,
    GDN_HEADS,
    GDN_HEADS,
    S5_WIDTH,
    LRU_WIDTH,
    LRU_WIDTH,
    RET_HEADS * RET_DK,
    RET_HEADS * RET_DK,
    RET_HEADS * RET_DV,
    RET_HEADS * RET_DV,
    N_BRANCH * D_MODEL,
)
N_IN = sum(IN_SPLITS)

kernel_name = "hybrid_gdn_s5_rglru_retnet_macaron"

F32 = jnp.float32


def rms_norm(x, w, eps=NORM_EPS):
    x32 = x.astype(F32)
    y = x32 * lax.rsqrt(jnp.mean(x32 * x32, axis=-1, keepdims=True) + eps)
    return (y * w.astype(F32)).astype(x.dtype)


def swiglu(x, w_gate, w_up, w_down):
    return (jax.nn.silu(x @ w_gate) * (x @ w_up)) @ w_down


def causal_dwconv(x, w):
    k = w.shape[0]
    return lax.conv_general_dilated(
        x, w[:, None, :].astype(x.dtype), window_strides=(1,), padding=[(k - 1, 0)],
        dimension_numbers=("NWC", "WIO", "NWC"), feature_group_count=x.shape[-1])


def l2norm(x, eps=1e-6):
    return x * lax.rsqrt(jnp.sum(x * x, axis=-1, keepdims=True) + eps)


def to_chunks(t, c):
    b, s, h, d = t.shape
    return t.reshape(b, s // c, c, h, d).transpose(0, 3, 1, 2, 4)


def from_chunks(t):
    b, h, n, c, d = t.shape
    return t.transpose(0, 2, 3, 1, 4).reshape(b, n * c, h, d)


def apply_rope(x, positions):
    half = x.shape[-1] // 2
    freqs = jnp.power(ROPE_BASE, -jnp.arange(half, dtype=F32) / half)
    ang = positions.astype(F32)[..., None] * freqs
    cos = jnp.cos(ang)[:, :, None, :]
    sin = jnp.sin(ang)[:, :, None, :]
    x1, x2 = x[..., :half], x[..., half:]
    return jnp.concatenate([x1 * cos - x2 * sin, x1 * sin + x2 * cos], axis=-1)


def gated_deltanet(qkv, z, beta_logit, a_logit, conv_w, a_log, dt_bias, norm_w):
    b, s, _ = qkv.shape
    dtype = qkv.dtype
    h, dk, dv, c = GDN_HEADS, GDN_DK, GDN_DV, GDN_CHUNK
    qkv = jax.nn.silu(causal_dwconv(qkv, conv_w)).astype(F32)
    q, k, v = jnp.split(qkv, [h * dk, 2 * h * dk], axis=-1)
    q = l2norm(q.reshape(b, s, h, dk)) * (dk ** -0.5)
    k = l2norm(k.reshape(b, s, h, dk))
    v = v.reshape(b, s, h, dv)
    beta = jax.nn.sigmoid(beta_logit.astype(F32))
    g = -jnp.exp(a_log.astype(F32)) * jax.nn.softplus(a_logit.astype(F32) + dt_bias.astype(F32))
    q, k, v = to_chunks(q, c), to_chunks(k, c), to_chunks(v, c)
    beta = to_chunks(beta[..., None], c)[..., 0]
    gc = jnp.cumsum(to_chunks(g[..., None], c)[..., 0], axis=-1)
    incl = jnp.tril(jnp.ones((c, c), dtype=bool))
    strict = jnp.tril(jnp.ones((c, c), dtype=bool), -1)
    diff = gc[..., :, None] - gc[..., None, :]
    decay = jnp.where(incl, jnp.exp(jnp.where(incl, diff, 0.0)), 0.0)
    kk = jnp.einsum("bhncd,bhnmd->bhncm", k, k)
    lower = jnp.where(strict, beta[..., None] * kk * decay, 0.0)
    rhs = jnp.concatenate([v * beta[..., None], k * (beta * jnp.exp(gc))[..., None]], axis=-1)
    sol = lax.linalg.triangular_solve(lower, rhs, left_side=True, lower=True, unit_diagonal=True)
    u, w = sol[..., :dv], sol[..., dv:]
    attn = jnp.einsum("bhncd,bhnmd->bhncm", q, k) * decay
    q_dec = q * jnp.exp(gc)[..., None]
    k_dec = k * jnp.exp(gc[..., -1:] - gc)[..., None]
    chunk_decay = jnp.exp(gc[..., -1])

    def step(state, xs):
        u_n, w_n, attn_n, q_n, k_n, dec_n = xs
        v_new = u_n - jnp.einsum("bhcd,bhde->bhce", w_n, state)
        o = jnp.einsum("bhcd,bhde->bhce", q_n, state) + jnp.einsum("bhcm,bhme->bhce", attn_n, v_new)
        state = state * dec_n[..., None, None] + jnp.einsum("bhcd,bhce->bhde", k_n, v_new)
        return state, o

    xs = tuple(jnp.moveaxis(t, 2, 0) for t in (u, w, attn, q_dec, k_dec, chunk_decay))
    _, o = lax.scan(step, jnp.zeros((b, h, dk, dv), F32), xs)
    o = from_chunks(jnp.moveaxis(o, 0, 2))
    o = rms_norm(o, norm_w) * jax.nn.silu(z.astype(F32).reshape(b, s, h, dv))
    return o.reshape(b, s, h * dv).astype(dtype)


def s5_layer(u, lam_re, lam_im, b_re, b_im, c_re, c_im, d, log_dt, w_glu, b_glu):
    bsz, s, _ = u.shape
    dtype = u.dtype
    gsz, gw, p = S5_GROUPS, S5_GROUP_WIDTH, S5_STATE
    u32 = u.astype(F32).reshape(bsz, s, gsz, gw)
    dt = jnp.exp(log_dt.astype(F32))[:, None]
    lr, li = lam_re.astype(F32), lam_im.astype(F32)
    mag = jnp.exp(lr * dt)
    abar_re, abar_im = mag * jnp.cos(li * dt), mag * jnp.sin(li * dt)
    den = lr * lr + li * li
    nr = abar_re - 1.0
    coef_re = (nr * lr + abar_im * li) / den
    coef_im = (abar_im * lr - nr * li) / den
    br, bi = b_re.astype(F32), b_im.astype(F32)
    bbar_re = coef_re[..., None] * br - coef_im[..., None] * bi
    bbar_im = coef_re[..., None] * bi + coef_im[..., None] * br
    bu_re = jnp.einsum("gpc,bsgc->bsgp", bbar_re, u32)
    bu_im = jnp.einsum("gpc,bsgc->bsgp", bbar_im, u32)
    a_re = jnp.broadcast_to(abar_re, (1, s, gsz, p))
    a_im = jnp.broadcast_to(abar_im, (1, s, gsz, p))

    def combine(e1, e2):
        a1r, a1i, b1r, b1i = e1
        a2r, a2i, b2r, b2i = e2
        return (a2r * a1r - a2i * a1i, a2r * a1i + a2i * a1r,
                a2r * b1r - a2i * b1i + b2r, a2r * b1i + a2i * b1r + b2i)

    _, _, x_re, x_im = lax.associative_scan(combine, (a_re, a_im, bu_re, bu_im), axis=1)
    y = (jnp.einsum("gcp,bsgp->bsgc", c_re.astype(F32), x_re)
         - jnp.einsum("gcp,bsgp->bsgc", c_im.astype(F32), x_im)
         + d.astype(F32) * u32)
    y = jax.nn.gelu(y.reshape(bsz, s, S5_WIDTH))
    y = y * jax.nn.sigmoid(y @ w_glu.astype(F32) + b_glu.astype(F32))
    return y.astype(dtype)


def rg_lru_block(xb, yb, conv_w, conv_b, w_a, b_a, w_i, b_i, lam):
    bsz, s, _ = xb.shape
    dtype = xb.dtype
    xc = (causal_dwconv(xb, conv_w) + conv_b).astype(F32)
    xblk = xc.reshape(bsz, s, LRU_BLOCKS, LRU_BLOCK_WIDTH)
    r = jax.nn.sigmoid(jnp.einsum("bsnc,ncd->bsnd", xblk, w_a.astype(F32)).reshape(bsz, s, LRU_WIDTH) + b_a.astype(F32))
    i = jax.nn.sigmoid(jnp.einsum("bsnc,ncd->bsnd", xblk, w_i.astype(F32)).reshape(bsz, s, LRU_WIDTH) + b_i.astype(F32))
    log_a = -LRU_C * r * jax.nn.softplus(-lam.astype(F32))
    a = jnp.exp(log_a)
    mult = jnp.sqrt(-jnp.expm1(2.0 * log_a))
    gated_x = xc * i * mult

    def step(h, inp):
        a_t, x_t = inp
        h = a_t * h + x_t
        return h, h

    _, hs = lax.scan(step, jnp.zeros((bsz, LRU_WIDTH), F32), (jnp.swapaxes(a, 0, 1), jnp.swapaxes(gated_x, 0, 1)))
    h = jnp.swapaxes(hs, 0, 1)
    return (h * jax.nn.gelu(yb.astype(F32))).astype(dtype)


def retention(q, k, v, g, positions, norm_w):
    bsz, s, _ = q.shape
    dtype = q.dtype
    h, dk, dv, c = RET_HEADS, RET_DK, RET_DV, RET_CHUNK
    q = apply_rope(q.astype(F32).reshape(bsz, s, h, dk), positions)
    k = apply_rope(k.astype(F32).reshape(bsz, s, h, dk), positions) * (dk ** -0.5)
    v = v.astype(F32).reshape(bsz, s, h, dv)
    log_gamma = jnp.log1p(-jnp.exp2(-5.0 - jnp.arange(h, dtype=F32)))
    idx = jnp.arange(c, dtype=F32)
    rel = idx[:, None] - idx[None, :]
    causal = rel >= 0
    intra = jnp.where(causal, jnp.exp(jnp.where(causal, rel, 0.0)[None] * log_gamma[:, None, None]), 0.0)
    qc, kc, vc = to_chunks(q, c), to_chunks(k, c), to_chunks(v, c)
    scores = jnp.einsum("bhncd,bhnmd->bhncm", qc, kc) * intra[None, :, None]
    o = jnp.einsum("bhncm,bhnme->bhnce", scores, vc)
    zeta = jnp.exp((c - 1.0 - idx)[None] * log_gamma[:, None])
    xi = jnp.exp((idx + 1.0)[None] * log_gamma[:, None])
    chunk_decay = jnp.exp(c * log_gamma)
    kv = jnp.einsum("bhncd,bhnce->bhnde", kc * zeta[None, :, None, :, None], vc)

    def step(state, kv_n):
        return state * chunk_decay[None, :, None, None] + kv_n, state

    _, prev = lax.scan(step, jnp.zeros((bsz, h, dk, dv), F32), jnp.moveaxis(kv, 2, 0))
    prev = jnp.moveaxis(prev, 0, 2)
    o = o + jnp.einsum("bhncd,bhnde->bhnce", qc, prev) * xi[None, :, None, :, None]
    o = from_chunks(o)
    mu = jnp.mean(o, axis=-1, keepdims=True)
    var = jnp.mean(jnp.square(o - mu), axis=-1, keepdims=True)
    on = ((o - mu) * lax.rsqrt(var + NORM_EPS)).reshape(bsz, s, h * dv) * norm_w.astype(F32)
    return (jax.nn.silu(g.astype(F32)) * on).astype(dtype)


def split_in(proj):
    offs, acc = [], 0
    for n in IN_SPLITS[:-1]:
        acc += n
        offs.append(acc)
    return jnp.split(proj, offs, axis=-1)


def hybrid_mixer(hn, positions, w_in, gdn_conv_w, gdn_a_log, gdn_dt_bias, gdn_norm_w,
                 s5_lambda_re, s5_lambda_im, s5_b_re, s5_b_im, s5_c_re, s5_c_im, s5_d, s5_log_dt,
                 s5_w_glu, s5_b_glu, lru_conv_w, lru_conv_b, lru_w_a, lru_b_a, lru_w_i, lru_b_i,
                 lru_lambda, ret_norm_w, w_branch, w_out):
    bsz, s, _ = hn.shape
    proj = hn @ w_in
    (gdn_qkv, gdn_z, gdn_b, gdn_a, s5_u, lru_x, lru_y,
     ret_q, ret_k, ret_v, ret_g, gate_logits) = split_in(proj)
    out_a = gated_deltanet(gdn_qkv, gdn_z, gdn_b, gdn_a, gdn_conv_w, gdn_a_log, gdn_dt_bias, gdn_norm_w)
    out_b = s5_layer(s5_u, s5_lambda_re, s5_lambda_im, s5_b_re, s5_b_im, s5_c_re, s5_c_im,
                     s5_d, s5_log_dt, s5_w_glu, s5_b_glu)
    out_c = rg_lru_block(lru_x, lru_y, lru_conv_w, lru_conv_b, lru_w_a, lru_b_a, lru_w_i, lru_b_i, lru_lambda)
    out_d = retention(ret_q, ret_k, ret_v, ret_g, positions, ret_norm_w)
    branches = jnp.stack([out_a, out_b, out_c, out_d], axis=2)
    gates = jax.nn.sigmoid(gate_logits.reshape(bsz, s, N_BRANCH, D_MODEL))
    merged = jnp.sum(gates * jnp.einsum("bsnc,ncd->bsnd", branches, w_branch), axis=2)
    return merged @ w_out


def setup_inputs(seed: int = 0) -> dict:
    key = jax.random.key(seed)
    ks = iter(jax.random.split(key, 48))
    L = DEPTH

    def nrm(shape, scale):
        return jax.random.normal(next(ks), shape, F32) * scale

    def unif(shape, lo, hi):
        return jax.random.uniform(next(ks), shape, F32, lo, hi)

    x = jax.random.normal(next(ks), (BATCH, SEQ, D_MODEL), F32)
    positions = jnp.broadcast_to(jnp.arange(SEQ, dtype=jnp.int32), (BATCH, SEQ))
    gdn_dt = jnp.exp(unif((L, GDN_HEADS), math.log(1e-3), math.log(1e-1)))
    lru_a0 = unif((L, LRU_WIDTH), 0.9, 0.999) ** (1.0 / LRU_C)
    return {
        "x": x,
        "positions": positions,
        "ffn1_norm": 1.0 + nrm((L, D_MODEL), 0.02),
        "ffn1_w_gate": nrm((L, D_MODEL, D_FF), D_MODEL ** -0.5),
        "ffn1_w_up": nrm((L, D_MODEL, D_FF), D_MODEL ** -0.5),
        "ffn1_w_down": nrm((L, D_FF, D_MODEL), D_FF ** -0.5),
        "mix_norm": 1.0 + nrm((L, D_MODEL), 0.02),
        "w_in": nrm((L, D_MODEL, N_IN), D_MODEL ** -0.5),
        "gdn_conv_w": nrm((L, GDN_CONV, GDN_HEADS * (2 * GDN_DK + GDN_DV)), GDN_CONV ** -0.5),
        "gdn_a_log": jnp.log(unif((L, GDN_HEADS), 1.0, 16.0)),
        "gdn_dt_bias": gdn_dt + jnp.log(-jnp.expm1(-gdn_dt)),
        "gdn_norm_w": 1.0 + nrm((L, GDN_DV), 0.02),
        "s5_lambda_re": -0.5 + nrm((L, S5_GROUPS, S5_STATE), 0.01),
        "s5_lambda_im": jnp.pi * jnp.arange(S5_STATE, dtype=F32) + nrm((L, S5_GROUPS, S5_STATE), 0.01),
        "s5_b_re": nrm((L, S5_GROUPS, S5_STATE, S5_GROUP_WIDTH), (2.0 * S5_GROUP_WIDTH) ** -0.5),
        "s5_b_im": nrm((L, S5_GROUPS, S5_STATE, S5_GROUP_WIDTH), (2.0 * S5_GROUP_WIDTH) ** -0.5),
        "s5_c_re": nrm((L, S5_GROUPS, S5_GROUP_WIDTH, S5_STATE), S5_STATE ** -0.5),
        "s5_c_im": nrm((L, S5_GROUPS, S5_GROUP_WIDTH, S5_STATE), S5_STATE ** -0.5),
        "s5_d": nrm((L, S5_GROUPS, S5_GROUP_WIDTH), 1.0),
        "s5_log_dt": unif((L, S5_GROUPS), math.log(1e-3), math.log(1e-1)),
        "s5_w_glu": nrm((L, S5_WIDTH, S5_WIDTH), S5_WIDTH ** -0.5),
        "s5_b_glu": nrm((L, S5_WIDTH), 0.01),
        "lru_conv_w": nrm((L, LRU_CONV, LRU_WIDTH), LRU_CONV ** -0.5),
        "lru_conv_b": nrm((L, LRU_WIDTH), 0.01),
        "lru_w_a": nrm((L, LRU_BLOCKS, LRU_BLOCK_WIDTH, LRU_BLOCK_WIDTH), LRU_BLOCK_WIDTH ** -0.5),
        "lru_b_a": nrm((L, LRU_WIDTH), 0.01),
        "lru_w_i": nrm((L, LRU_BLOCKS, LRU_BLOCK_WIDTH, LRU_BLOCK_WIDTH), LRU_BLOCK_WIDTH ** -0.5),
        "lru_b_i": nrm((L, LRU_WIDTH), 0.01),
        "lru_lambda": jnp.log(lru_a0) - jnp.log1p(-lru_a0),
        "ret_norm_w": 1.0 + nrm((L, RET_HEADS * RET_DV), 0.02),
        "w_branch": nrm((L, N_BRANCH, BRANCH_WIDTH, D_MODEL), BRANCH_WIDTH ** -0.5),
        "w_out": nrm((L, D_MODEL, D_MODEL), D_MODEL ** -0.5),
        "ffn2_norm": 1.0 + nrm((L, D_MODEL), 0.02),
        "ffn2_w_gate": nrm((L, D_MODEL, D_FF), D_MODEL ** -0.5),
        "ffn2_w_up": nrm((L, D_MODEL, D_FF), D_MODEL ** -0.5),
        "ffn2_w_down": nrm((L, D_FF, D_MODEL), D_FF ** -0.5),
        "final_norm": 1.0 + nrm((D_MODEL,), 0.02),
    }


def reference(x, positions, ffn1_norm, ffn1_w_gate, ffn1_w_up, ffn1_w_down, mix_norm, w_in,
              gdn_conv_w, gdn_a_log, gdn_dt_bias, gdn_norm_w,
              s5_lambda_re, s5_lambda_im, s5_b_re, s5_b_im, s5_c_re, s5_c_im, s5_d, s5_log_dt,
              s5_w_glu, s5_b_glu, lru_conv_w, lru_conv_b, lru_w_a, lru_b_a, lru_w_i, lru_b_i,
              lru_lambda, ret_norm_w, w_branch, w_out,
              ffn2_norm, ffn2_w_gate, ffn2_w_up, ffn2_w_down, final_norm):
    for l in range(DEPTH):
        x = x + 0.5 * swiglu(rms_norm(x, ffn1_norm[l]), ffn1_w_gate[l], ffn1_w_up[l], ffn1_w_down[l])
        x = x + hybrid_mixer(
            rms_norm(x, mix_norm[l]), positions, w_in[l],
            gdn_conv_w[l], gdn_a_log[l], gdn_dt_bias[l], gdn_norm_w[l],
            s5_lambda_re[l], s5_lambda_im[l], s5_b_re[l], s5_b_im[l], s5_c_re[l], s5_c_im[l],
            s5_d[l], s5_log_dt[l], s5_w_glu[l], s5_b_glu[l],
            lru_conv_w[l], lru_conv_b[l], lru_w_a[l], lru_b_a[l], lru_w_i[l], lru_b_i[l], lru_lambda[l],
            ret_norm_w[l], w_branch[l], w_out[l])
        x = x + 0.5 * swiglu(rms_norm(x, ffn2_norm[l]), ffn2_w_gate[l], ffn2_w_up[l], ffn2_w_down[l])
    return rms_norm(x, final_norm)
```

```python
import functools
import math

import numpy as np
import jax
import jax.numpy as jnp
from jax import lax
from jax.experimental import pallas as pl
from jax.experimental.pallas import tpu as pltpu

F32 = jnp.float32
BF16 = jnp.bfloat16

D_MODEL = 1024
D_FF = 2816
NORM_EPS = 1e-6
N_BRANCH = 4
HEADS = 4
HEAD_DIM = 64
GDN_CONV = 4
GDN_CHUNK = 64
S5_GROUPS = 16
S5_GROUP_WIDTH = 16
S5_STATE = 64
S5_WIDTH = S5_GROUPS * S5_GROUP_WIDTH
S5_NSTATE = S5_GROUPS * S5_STATE
LRU_WIDTH = 256
LRU_BLOCKS = 4
LRU_CONV = 4
LRU_C = 8.0
ROPE_BASE = 10000.0
BW = 256

LANES = 128
HALO = 8

SEQ_TILE = 256
TOK_TILE = 512
FF_CHUNK = 256
VMEM_LIMIT = 56 * 1024 * 1024

C_QKV = 0
C_Z = C_QKV + 3 * BW
C_BETA = C_Z + BW
C_A = C_BETA + HEADS * LANES
C_S5 = C_A + HEADS * LANES
C_LX = C_S5 + BW
C_LY = C_LX + BW
C_RQ = C_LY + BW
C_RK = C_RQ + BW
C_RV = C_RK + BW
C_RG = C_RV + BW
N_A = C_RG + BW


def _sigmoid(x):
    return 1.0 / (1.0 + jnp.exp(-x))


def _silu(x):
    return x * _sigmoid(x)


def _gelu_tanh(x):
    return 0.5 * x * (1.0 + jnp.tanh(math.sqrt(2.0 / math.pi) * (x + 0.044715 * (x * x * x))))


def _softplus(x):
    return jnp.maximum(x, 0.0) + jnp.log(1.0 + jnp.exp(-jnp.abs(x)))


def _rms(x, w):
    return x * lax.rsqrt(jnp.mean(x * x, axis=-1, keepdims=True) + NORM_EPS) * w


def _dot(a, b):
    return jnp.dot(a.astype(BF16), b.astype(BF16), preferred_element_type=F32)


def _dot_nt(a, b):
    return lax.dot_general(a.astype(BF16), b.astype(BF16), (((1,), (1,)), ((), ())),
                           preferred_element_type=F32)


def _dot_tn(a, b):
    return lax.dot_general(a.astype(BF16), b.astype(BF16), (((0,), (0,)), ((), ())),
                           preferred_element_type=F32)


def _split_bf16(x, n):
    terms = []
    for _ in range(n - 1):
        t = x.astype(BF16)
        terms.append(t)
        x = x - t.astype(F32)
    terms.append(x.astype(BF16))
    return terms


def _dot_wide_rhs(m_bf16, x, n):
    return sum(jnp.dot(m_bf16, t, preferred_element_type=F32) for t in _split_bf16(x, n))


def _dot_wide_lhs(x, m_bf16, n):
    return sum(jnp.dot(t, m_bf16, preferred_element_type=F32) for t in _split_bf16(x, n))


def _iota2(shape, axis):
    return lax.broadcasted_iota(jnp.int32, shape, axis)


def _head_of(i):
    return lax.shift_right_logical(i, 6)


def _qhead_of(i):
    return lax.shift_right_logical(lax.bitwise_and(i, LANES - 1), 5)


def _ffn_kernel(x_ref, nw_ref, wg_ref, wu_ref, wd_ref, fw_ref, o_ref, *, final):
    x = x_ref[...]
    hn = _rms(x, nw_ref[...]).astype(BF16)
    acc = jnp.zeros(x.shape, F32)
    for c in range(D_FF // FF_CHUNK):
        cs = slice(c * FF_CHUNK, (c + 1) * FF_CHUNK)
        g = jnp.dot(hn, wg_ref[:, cs], preferred_element_type=F32)
        u = jnp.dot(hn, wu_ref[:, cs], preferred_element_type=F32)
        a = (_silu(g) * u).astype(BF16)
        acc = acc + jnp.dot(a, wd_ref[cs, :], preferred_element_type=F32)
    y = x + 0.5 * acc
    if final:
        y = _rms(y, fw_ref[...])
    o_ref[...] = y


def _ffn(x2, layer, nw, wg, wu, wd, fw, final):
    t = x2.shape[0]
    tm = min(TOK_TILE, t)
    wspec = lambda shape: pl.BlockSpec((None,) + shape, lambda i: (layer, 0, 0),
                                       pipeline_mode=pl.Buffered(1))
    return pl.pallas_call(
        functools.partial(_ffn_kernel, final=final),
        grid=(t // tm,),
        in_specs=[
            pl.BlockSpec((tm, D_MODEL), lambda i: (i, 0)),
            pl.BlockSpec((None, 1, D_MODEL), lambda i: (layer, 0, 0)),
            wspec((D_MODEL, D_FF)),
            wspec((D_MODEL, D_FF)),
            wspec((D_FF, D_MODEL)),
            pl.BlockSpec((1, D_MODEL), lambda i: (0, 0)),
        ],
        out_specs=pl.BlockSpec((tm, D_MODEL), lambda i: (i, 0)),
        out_shape=jax.ShapeDtypeStruct((t, D_MODEL), F32),
        compiler_params=pltpu.CompilerParams(
            dimension_semantics=("arbitrary",), vmem_limit_bytes=VMEM_LIMIT),
        name="ffn",
    )(x2, nw, wg, wu, wd, fw)


def _merge_kernel(x_ref, br_ref, nw_ref, wgate_ref, wbr_ref, wout_ref, o_ref):
    x = x_ref[...]
    hn = _rms(x, nw_ref[...]).astype(BF16)
    merged = jnp.zeros(x.shape, F32)
    for n in range(N_BRANCH):
        gl = jnp.dot(hn, wgate_ref[:, n * D_MODEL:(n + 1) * D_MODEL], preferred_element_type=F32)
        bp = jnp.dot(br_ref[:, n * BW:(n + 1) * BW], wbr_ref[n], preferred_element_type=F32)
        merged = merged + _sigmoid(gl) * bp
    o_ref[...] = x + jnp.dot(merged.astype(BF16), wout_ref[...], preferred_element_type=F32)


def _merge(x2, br, layer, nw, wgate, wbr, wout):
    t = x2.shape[0]
    tm = min(TOK_TILE, t)
    return pl.pallas_call(
        _merge_kernel,
        grid=(t // tm,),
        in_specs=[
            pl.BlockSpec((tm, D_MODEL), lambda i: (i, 0)),
            pl.BlockSpec((tm, N_BRANCH * BW), lambda i: (i, 0)),
            pl.BlockSpec((None, 1, D_MODEL), lambda i: (layer, 0, 0)),
            pl.BlockSpec((None, D_MODEL, N_BRANCH * D_MODEL), lambda i: (layer, 0, 0),
                         pipeline_mode=pl.Buffered(1)),
            pl.BlockSpec((None, N_BRANCH, BW, D_MODEL), lambda i: (layer, 0, 0, 0),
                         pipeline_mode=pl.Buffered(1)),
            pl.BlockSpec((None, D_MODEL, D_MODEL), lambda i: (layer, 0, 0),
                         pipeline_mode=pl.Buffered(1)),
        ],
        out_specs=pl.BlockSpec((tm, D_MODEL), lambda i: (i, 0)),
        out_shape=jax.ShapeDtypeStruct((t, D_MODEL), F32),
        compiler_params=pltpu.CompilerParams(
            dimension_semantics=("arbitrary",), vmem_limit_bytes=VMEM_LIMIT),
        name="merge",
    )(x2, br, nw, wgate, wbr, wout)


def _rope_kernel(pos_ref, freq_ref, cos_ref, sin_ref):
    ang = pos_ref[...] * freq_ref[...]
    cos_ref[...] = jnp.cos(ang)
    sin_ref[...] = jnp.sin(ang)


def _rope_tables(pos_f32):
    t = pos_f32.shape[0]
    rows = min(1024, t)
    half = HEAD_DIM // 2
    freq = np.power(ROPE_BASE, -np.arange(half, dtype=np.float32) / half).astype(np.float32)
    freq = jnp.asarray(np.tile(freq, LANES // half)[None, :])
    return pl.pallas_call(
        _rope_kernel,
        grid=(t // rows,),
        in_specs=[pl.BlockSpec((rows, 1), lambda i: (i, 0)),
                  pl.BlockSpec((1, LANES), lambda i: (0, 0))],
        out_specs=[pl.BlockSpec((rows, LANES), lambda i: (i, 0)),
                   pl.BlockSpec((rows, LANES), lambda i: (i, 0))],
        out_shape=[jax.ShapeDtypeStruct((t, LANES), F32)] * 2,
        name="rope_tables",
    )(pos_f32, freq)


def _s5_prep_kernel(lr_ref, li_ref, ldt_ref, bre_ref, bim_ref, cre_ref, cim_ref,
                    bbar_ref, cmat_ref, ap_ref, *, nsteps):
    lr, li = lr_ref[...], li_ref[...]
    dt = jnp.exp(ldt_ref[...])
    mag = jnp.exp(lr * dt)
    ar, ai = mag * jnp.cos(li * dt), mag * jnp.sin(li * dt)
    den = lr * lr + li * li
    nr = ar - 1.0
    cr = (nr * lr + ai * li) / den
    ci = (ai * lr - nr * li) / den
    bre, bim = bre_ref[...], bim_ref[...]
    bbar_ref[:, 0:S5_NSTATE] = (cr * bre - ci * bim).astype(BF16)
    bbar_ref[:, S5_NSTATE:] = (cr * bim + ci * bre).astype(BF16)
    cmat_ref[0:S5_NSTATE, :] = cre_ref[...].astype(BF16)
    cmat_ref[S5_NSTATE:, :] = (-cim_ref[...]).astype(BF16)
    pr, pi = ar, ai
    for k in range(nsteps):
        ap_ref[k:k + 1, 0:S5_NSTATE] = pr
        ap_ref[k:k + 1, S5_NSTATE:] = pi
        pr, pi = pr * pr - pi * pi, 2.0 * pr * pi
    for k in range(nsteps, ap_ref.shape[0]):
        ap_ref[k:k + 1, :] = jnp.zeros((1, 2 * S5_NSTATE), F32)


def _s5_prep(lr, li, ldt, bre, bim, cre, cim, nsteps):
    nl = lr.shape[0]
    nrows = 8 * ((nsteps + 7) // 8)
    vec = pl.BlockSpec((None, 1, S5_NSTATE), lambda l: (l, 0, 0))
    return pl.pallas_call(
        functools.partial(_s5_prep_kernel, nsteps=nsteps),
        grid=(nl,),
        in_specs=[vec, vec, vec,
                  pl.BlockSpec((None, S5_WIDTH, S5_NSTATE), lambda l: (l, 0, 0)),
                  pl.BlockSpec((None, S5_WIDTH, S5_NSTATE), lambda l: (l, 0, 0)),
                  pl.BlockSpec((None, S5_NSTATE, S5_WIDTH), lambda l: (l, 0, 0)),
                  pl.BlockSpec((None, S5_NSTATE, S5_WIDTH), lambda l: (l, 0, 0))],
        out_specs=[pl.BlockSpec((None, S5_WIDTH, 2 * S5_NSTATE), lambda l: (l, 0, 0)),
                   pl.BlockSpec((None, 2 * S5_NSTATE, S5_WIDTH), lambda l: (l, 0, 0)),
                   pl.BlockSpec((None, nrows, 2 * S5_NSTATE), lambda l: (l, 0, 0))],
        out_shape=[jax.ShapeDtypeStruct((nl, S5_WIDTH, 2 * S5_NSTATE), BF16),
                   jax.ShapeDtypeStruct((nl, 2 * S5_NSTATE, S5_WIDTH), BF16),
                   jax.ShapeDtypeStruct((nl, nrows, 2 * S5_NSTATE), F32)],
        name="s5_prep",
    )(lr, li, ldt, bre, bim, cre, cim)


def _branch_kernel(
        x_ref, cos_ref, sin_ref, nw_ref, win_ref,
        gconv_ref, galog_ref, gdtb_ref, gnorm_ref,
        bbar_ref, cmat_ref, ap_ref, s5d_ref, wglu_ref, bglu_ref,
        lcw_ref, lcb_ref, lwa_ref, lba_ref, lwi_ref, lbi_ref, llam_ref,
        rnorm_ref, rintra_ref, rxi_ref, rzeta_ref, rdec_ref,
        o_ref,
        gbuf, lbuf, qkv_s, gcb_s, og_s, xs_s, gstate, s5c, lruc, rstate,
        *, ts, nsteps):
    first = pl.program_id(1) == 0

    @pl.when(first)
    def _():
        gbuf[0:HALO, :] = jnp.zeros((HALO, 3 * BW), F32)
        lbuf[0:HALO, :] = jnp.zeros((HALO, BW), F32)
        gstate[...] = jnp.zeros(gstate.shape, F32)
        rstate[...] = jnp.zeros(rstate.shape, F32)
        s5c[...] = jnp.zeros(s5c.shape, F32)
        lruc[...] = jnp.zeros(lruc.shape, F32)

    hn = _rms(x_ref[...], nw_ref[...]).astype(BF16)

    def proj(c0, width):
        return jnp.dot(hn, win_ref[:, c0:c0 + width], preferred_element_type=F32)

    r2 = _iota2((BW, BW), 0)
    c2 = _iota2((BW, BW), 1)
    same_head = _head_of(r2) == _head_of(c2)
    ones_bd = jnp.where(same_head, 1.0, 0.0).astype(BF16)
    lane = _iota2((1, BW), 1)

    def head_sum(v):
        return _dot_wide_lhs(v, ones_bd, 2)

    gbuf[HALO:HALO + ts, :] = proj(C_QKV, 3 * BW)
    conv = jnp.zeros((ts, 3 * BW), F32)
    for k in range(GDN_CONV):
        off = HALO - (GDN_CONV - 1) + k
        conv = conv + gconv_ref[k:k + 1, :] * gbuf[off:off + ts, :]
    gbuf[0:HALO, :] = gbuf[ts:ts + HALO, :]
    qkv = _silu(conv)
    q, k_, v = qkv[:, 0:BW], qkv[:, BW:2 * BW], qkv[:, 2 * BW:3 * BW]
    q = q * lax.rsqrt(head_sum(q * q) + 1e-6) * (HEAD_DIM ** -0.5)
    k_ = k_ * lax.rsqrt(head_sum(k_ * k_) + 1e-6)
    qkv_s[:, 0:BW] = q
    qkv_s[:, BW:2 * BW] = k_
    qkv_s[:, 2 * BW:3 * BW] = v

    beta = _sigmoid(proj(C_BETA, HEADS * LANES))
    g = -jnp.exp(galog_ref[...]) * _softplus(proj(C_A, HEADS * LANES) + gdtb_ref[...])
    rt = _iota2((ts, ts), 0)
    ct = _iota2((ts, ts), 1)
    mchunk = jnp.where((_head_of(rt) == _head_of(ct)) & (ct <= rt), 1.0, 0.0).astype(BF16)
    gcb_s[:, 0:HEADS * LANES] = _dot_wide_rhs(mchunk, g, 3)
    gcb_s[:, HEADS * LANES:] = beta

    incl = same_head & (c2 <= r2)
    strict = same_head & (c2 < r2)
    eye = jnp.where(r2 == c2, 1.0, 0.0)
    hmask = [jnp.where(_head_of(lane) == h, 1.0, 0.0) for h in range(HEADS)]

    def stack(m):
        return jnp.concatenate([m * hmask[h] for h in range(HEADS)], axis=0)

    def unstack(m):
        c = GDN_CHUNK
        return m[0:c] + m[c:2 * c] + m[2 * c:3 * c] + m[3 * c:4 * c]

    def gdn_chunk(n, carry):
        r0 = pl.multiple_of(n * GDN_CHUNK, GDN_CHUNK)
        rows = pl.ds(r0, GDN_CHUNK)
        qc = qkv_s[rows, 0:BW]
        kc = qkv_s[rows, BW:2 * BW]
        vc = qkv_s[rows, 2 * BW:3 * BW]
        gh = [gcb_s[rows, h * LANES:(h + 1) * LANES] for h in range(HEADS)]
        bh = [gcb_s[rows, (HEADS + h) * LANES:(HEADS + h + 1) * LANES] for h in range(HEADS)]
        gcol1 = jnp.concatenate(gh, axis=0)
        bcol1 = jnp.concatenate(bh, axis=0)
        gcol = jnp.concatenate([gcol1, gcol1], axis=1)
        bcol = jnp.concatenate([bcol1, bcol1], axis=1)
        grow1 = gcol1.T
        grow = jnp.concatenate([grow1, grow1], axis=0)
        decay = jnp.where(incl, jnp.exp(jnp.where(incl, gcol - grow, 0.0)), 0.0)
        ks, qs, vs = stack(kc), stack(qc), stack(vc)
        ks_b = ks.astype(BF16)
        kk = _dot_nt(ks_b, ks_b)
        low = jnp.where(strict, bcol * kk * decay, 0.0)
        tinv = eye - low
        p = low
        for _ in range(5):
            pb = p.astype(BF16)
            p = jnp.dot(pb, pb, preferred_element_type=F32)
            tinv = tinv + _dot(tinv, p)
        tinv_b = tinv.astype(BF16)
        u = unstack(_dot(tinv_b, vs * bcol))
        w = unstack(_dot(tinv_b, ks * (bcol * jnp.exp(gcol))))
        attn = _dot_nt(qs, ks_b) * decay
        g64 = jnp.concatenate(
            [jnp.where(_iota2((GDN_CHUNK, LANES), 1) < HEAD_DIM, gh[0], gh[1]),
             jnp.where(_iota2((GDN_CHUNK, LANES), 1) < HEAD_DIM, gh[2], gh[3])], axis=1)
        glast = g64[GDN_CHUNK - 1:GDN_CHUNK, :]
        q_dec = qc * jnp.exp(g64)
        k_dec = kc * jnp.exp(glast - g64)
        st = gstate[...]
        st_b = st.astype(BF16)
        v_new = u - _dot(w, st_b)
        o = _dot(q_dec, st_b) + unstack(_dot(attn, stack(v_new)))
        gstate[...] = st * jnp.exp(glast) + jnp.where(same_head, _dot_tn(k_dec, v_new), 0.0)
        og_s[rows, :] = o
        return carry

    lax.fori_loop(0, ts // GDN_CHUNK, gdn_chunk, 0)
    og = og_s[...]
    og = og * lax.rsqrt(head_sum(og * og) * (1.0 / HEAD_DIM) + NORM_EPS) * gnorm_ref[...]
    o_ref[:, 0:BW] = (og * _silu(proj(C_Z, BW))).astype(BF16)

    u5 = proj(C_S5, BW)
    xs_s[...] = _dot(u5, bbar_ref[...])
    row = _iota2((ts, LANES), 0)

    def s5_slab(j, carry):
        lr_ = pl.ds(pl.multiple_of(j * LANES, LANES), LANES)
        li_ = pl.ds(pl.multiple_of(S5_NSTATE + j * LANES, LANES), LANES)
        xr, xi = xs_s[:, lr_], xs_s[:, li_]
        ar, ai = ap_ref[0:1, lr_], ap_ref[0:1, li_]
        cr, ci = s5c[0:1, lr_], s5c[0:1, li_]
        xr = xr + jnp.where(row == 0, ar * cr - ai * ci, 0.0)
        xi = xi + jnp.where(row == 0, ar * ci + ai * cr, 0.0)
        for s in range(nsteps):
            d = 1 << s
            pr, pi = ap_ref[s:s + 1, lr_], ap_ref[s:s + 1, li_]
            sr = jnp.where(row >= d, pltpu.roll(xr, d, 0), 0.0)
            si = jnp.where(row >= d, pltpu.roll(xi, d, 0), 0.0)
            xr, xi = xr + (pr * sr - pi * si), xi + (pr * si + pi * sr)
        xs_s[:, lr_] = xr
        xs_s[:, li_] = xi
        s5c[0:1, lr_] = xr[ts - 1:ts, :]
        s5c[0:1, li_] = xi[ts - 1:ts, :]
        return carry

    lax.fori_loop(0, S5_NSTATE // LANES, s5_slab, 0)
    y5 = _dot(xs_s[...], cmat_ref[...]) + s5d_ref[...] * u5
    y5 = _gelu_tanh(y5)
    y5 = y5 * _sigmoid(_dot(y5, wglu_ref[...]) + bglu_ref[...])
    o_ref[:, BW:2 * BW] = y5.astype(BF16)

    lbuf[HALO:HALO + ts, :] = proj(C_LX, BW)
    xc = jnp.zeros((ts, BW), F32) + lcb_ref[...]
    for k in range(LRU_CONV):
        off = HALO - (LRU_CONV - 1) + k
        xc = xc + lcw_ref[k:k + 1, :] * lbuf[off:off + ts, :]
    lbuf[0:HALO, :] = lbuf[ts:ts + HALO, :]
    rg = _sigmoid(_dot(xc, lwa_ref[...]) + lba_ref[...])
    ig = _sigmoid(_dot(xc, lwi_ref[...]) + lbi_ref[...])
    log_a = -LRU_C * rg * _softplus(-llam_ref[...])
    a = jnp.exp(log_a)
    mult = jnp.sqrt(1.0 - jnp.exp(2.0 * log_a))
    hx = xc * ig * mult
    rowl = _iota2((ts, BW), 0)
    hx = hx + jnp.where(rowl == 0, a * lruc[0:1, :], 0.0)
    for s in range(nsteps):
        d = 1 << s
        sa = jnp.where(rowl >= d, pltpu.roll(a, d, 0), 1.0)
        sx = jnp.where(rowl >= d, pltpu.roll(hx, d, 0), 0.0)
        hx = hx + a * sx
        a = a * sa
    lruc[0:1, :] = hx[ts - 1:ts, :]
    o_ref[:, 2 * BW:3 * BW] = (hx * _gelu_tanh(proj(C_LY, BW))).astype(BF16)

    cos, sin = cos_ref[...], sin_ref[...]

    def rope(t):
        t1, t2 = t[:, 0:LANES], t[:, LANES:]
        return jnp.concatenate([t1 * cos - t2 * sin, t1 * sin + t2 * cos], axis=1)

    rq = rope(proj(C_RQ, BW))
    rk = rope(proj(C_RK, BW)) * (HEAD_DIM ** -0.5)
    rv = proj(C_RV, BW)
    rv_b = rv.astype(BF16)
    rk_b = rk.astype(BF16)
    qmask = [jnp.where(_qhead_of(lane) == h, 1.0, 0.0) for h in range(HEADS)]
    orr = _dot(rq * rxi_ref[...], rstate[...])
    for h in range(HEADS):
        sc = _dot_nt(rq * qmask[h], rk_b) * rintra_ref[h]
        orr = orr + hmask[h] * _dot(sc, rv_b)
    qv_same = _qhead_of(r2) == _head_of(c2)
    rstate[...] = rstate[...] * rdec_ref[...] + jnp.where(
        qv_same, _dot_tn(rk * rzeta_ref[...], rv_b), 0.0)
    mu = head_sum(orr) * (1.0 / HEAD_DIM)
    cen = orr - mu
    var = head_sum(cen * cen) * (1.0 / HEAD_DIM)
    on = cen * lax.rsqrt(var + NORM_EPS) * rnorm_ref[...]
    o_ref[:, 3 * BW:4 * BW] = (_silu(proj(C_RG, BW)) * on).astype(BF16)


def _retention_tables(ts):
    hh = np.arange(HEADS, dtype=np.float64)
    log_gamma = np.log1p(-np.exp2(-5.0 - hh))
    idx = np.arange(ts, dtype=np.float64)
    rel = idx[:, None] - idx[None, :]
    intra = np.where(rel >= 0, np.exp(np.where(rel >= 0, rel, 0.0)[None] * log_gamma[:, None, None]), 0.0)
    xi = np.exp((idx + 1.0)[None] * log_gamma[:, None])
    zeta = np.exp((ts - 1.0 - idx)[None] * log_gamma[:, None])
    cdec = np.exp(ts * log_gamma)
    qhead = (np.arange(BW) % LANES) // (HEAD_DIM // 2)
    vhead = np.arange(BW) // HEAD_DIM
    xi_q = xi[qhead].T
    zeta_q = zeta[qhead].T
    dec = np.where(qhead[:, None] == vhead[None, :], cdec[qhead][:, None], 0.0)
    f = lambda a: jnp.asarray(a.astype(np.float32))
    return f(intra), f(xi_q), f(zeta_q), f(dec)


def _branches(x3, cos3, sin3, layer, p, ts):
    b, s, _ = x3.shape
    nsteps = int(math.log2(ts))
    assert 1 << nsteps == ts and s % ts == 0 and ts % GDN_CHUNK == 0
    rintra, rxi, rzeta, rdec = _retention_tables(ts)

    def lspec(shape, single=False):
        nd = len(shape)
        kw = {"pipeline_mode": pl.Buffered(1)} if single else {}
        return pl.BlockSpec((None,) + shape, lambda bi, si: (layer,) + (0,) * nd, **kw)

    def cspec(shape):
        nd = len(shape)
        return pl.BlockSpec(shape, lambda bi, si: (0,) * nd)

    tile = lambda w: pl.BlockSpec((None, ts, w), lambda bi, si: (bi, si, 0))
    in_specs = [
        tile(D_MODEL), tile(LANES), tile(LANES),
        lspec((1, D_MODEL)), lspec((D_MODEL, N_A), single=True),
        lspec((GDN_CONV, 3 * BW)), lspec((1, HEADS * LANES)), lspec((1, HEADS * LANES)), lspec((1, BW)),
        lspec((S5_WIDTH, 2 * S5_NSTATE)), lspec((2 * S5_NSTATE, S5_WIDTH)),
        lspec((p["s5_ap"].shape[1], 2 * S5_NSTATE)), lspec((1, BW)), lspec((BW, BW)), lspec((1, BW)),
        lspec((LRU_CONV, BW)), lspec((1, BW)), lspec((BW, BW)), lspec((1, BW)), lspec((BW, BW)),
        lspec((1, BW)), lspec((1, BW)),
        lspec((1, BW)), cspec((HEADS, ts, ts)), cspec((ts, BW)), cspec((ts, BW)), cspec((BW, BW)),
    ]
    scratch = [
        pltpu.VMEM((ts + HALO, 3 * BW), F32),
        pltpu.VMEM((ts + HALO, BW), F32),
        pltpu.VMEM((ts, 3 * BW), F32),
        pltpu.VMEM((ts, 2 * HEADS * LANES), F32),
        pltpu.VMEM((ts, BW), F32),
        pltpu.VMEM((ts, 2 * S5_NSTATE), F32),
        pltpu.VMEM((BW, BW), F32),
        pltpu.VMEM((HALO, 2 * S5_NSTATE), F32),
        pltpu.VMEM((HALO, BW), F32),
        pltpu.VMEM((BW, BW), F32),
    ]
    return pl.pallas_call(
        functools.partial(_branch_kernel, ts=ts, nsteps=nsteps),
        grid=(b, s // ts),
        in_specs=in_specs,
        out_specs=pl.BlockSpec((None, ts, N_BRANCH * BW), lambda bi, si: (bi, si, 0)),
        out_shape=jax.ShapeDtypeStruct((b, s, N_BRANCH * BW), BF16),
        scratch_shapes=scratch,
        compiler_params=pltpu.CompilerParams(
            dimension_semantics=("arbitrary", "arbitrary"), vmem_limit_bytes=VMEM_LIMIT),
        name="branches",
    )(x3, cos3, sin3, p["mix_norm"], p["w_in_a"],
      p["gdn_conv_w"], p["gdn_a_log"], p["gdn_dt_bias"], p["gdn_norm_w"],
      p["s5_bbar"], p["s5_cmat"], p["s5_ap"], p["s5_d"], p["s5_w_glu"], p["s5_b_glu"],
      p["lru_conv_w"], p["lru_conv_b"], p["lru_w_a"], p["lru_b_a"], p["lru_w_i"], p["lru_b_i"],
      p["lru_lambda"], p["ret_norm_w"], rintra, rxi, rzeta, rdec)


def _in_proj_columns():
    offs = np.cumsum([0, 3 * BW, BW, HEADS, HEADS, BW, BW, BW, BW, BW, BW, BW])
    qkv, z, beta, a, s5, lx, ly, rq, rk, rv, rg = offs[:11]
    p = np.arange(LANES)
    rope_perm = np.concatenate([(p // 32) * HEAD_DIM + p % 32, (p // 32) * HEAD_DIM + 32 + p % 32])
    cols = np.concatenate([
        qkv + np.arange(3 * BW), z + np.arange(BW),
        beta + np.repeat(np.arange(HEADS), LANES), a + np.repeat(np.arange(HEADS), LANES),
        s5 + np.arange(BW), lx + np.arange(BW), ly + np.arange(BW),
        rq + rope_perm, rk + rope_perm, rv + np.arange(BW), rg + np.arange(BW)])
    assert cols.shape[0] == N_A
    return cols, int(offs[11])


def _block_diag(blocks):
    nl, n, r, c = blocks.shape
    eye = jnp.eye(n, dtype=blocks.dtype)
    return (blocks[:, :, :, None, :] * eye[None, :, None, :, None]).reshape(nl, n * r, n * c)


def _prepare(prm, ts):
    nl = prm["w_in"].shape[0]
    cols, gate0 = _in_proj_columns()
    row = lambda a: a.reshape(nl, 1, -1)
    rep = lambda a, n: jnp.repeat(a, n, axis=-1).reshape(nl, 1, -1)
    p = {}
    p["w_in_a"] = jnp.take(prm["w_in"], jnp.asarray(cols), axis=2).astype(BF16)
    p["w_gate"] = prm["w_in"][:, :, gate0:].astype(BF16)
    p["mix_norm"] = row(prm["mix_norm"])
    p["gdn_conv_w"] = prm["gdn_conv_w"]
    p["gdn_a_log"] = rep(prm["gdn_a_log"], LANES)
    p["gdn_dt_bias"] = rep(prm["gdn_dt_bias"], LANES)
    p["gdn_norm_w"] = jnp.tile(prm["gdn_norm_w"], (1, HEADS)).reshape(nl, 1, BW)
    lr = row(prm["s5_lambda_re"])
    li = row(prm["s5_lambda_im"])
    ldt = rep(prm["s5_log_dt"], S5_STATE)
    bre = _block_diag(jnp.swapaxes(prm["s5_b_re"], 2, 3))
    bim = _block_diag(jnp.swapaxes(prm["s5_b_im"], 2, 3))
    cre = _block_diag(jnp.swapaxes(prm["s5_c_re"], 2, 3))
    cim = _block_diag(jnp.swapaxes(prm["s5_c_im"], 2, 3))
    p["s5_bbar"], p["s5_cmat"], p["s5_ap"] = _s5_prep(lr, li, ldt, bre, bim, cre, cim, int(math.log2(ts)))
    p["s5_d"] = row(prm["s5_d"])
    p["s5_w_glu"] = prm["s5_w_glu"].astype(BF16)
    p["s5_b_glu"] = row(prm["s5_b_glu"])
    p["lru_conv_w"] = prm["lru_conv_w"]
    p["lru_conv_b"] = row(prm["lru_conv_b"])
    p["lru_w_a"] = _block_diag(prm["lru_w_a"]).astype(BF16)
    p["lru_w_i"] = _block_diag(prm["lru_w_i"]).astype(BF16)
    p["lru_b_a"] = row(prm["lru_b_a"])
    p["lru_b_i"] = row(prm["lru_b_i"])
    p["lru_lambda"] = row(prm["lru_lambda"])
    p["ret_norm_w"] = row(prm["ret_norm_w"])
    p["w_branch"] = prm["w_branch"].astype(BF16)
    p["w_out"] = prm["w_out"].astype(BF16)
    for f in ("ffn1", "ffn2"):
        p[f + "_norm"] = row(prm[f + "_norm"])
        for w in ("w_gate", "w_up", "w_down"):
            p[f + "_" + w] = prm[f + "_" + w].astype(BF16)
    return p


def _trunk(x, positions, prm, final_norm, ts):
    b, s, d = x.shape
    nl = prm["w_in"].shape[0]
    p = _prepare(prm, ts)
    cos, sin = _rope_tables(positions.astype(F32).reshape(b * s, 1))
    cos3, sin3 = cos.reshape(b, s, LANES), sin.reshape(b, s, LANES)
    fw = final_norm.reshape(1, d)
    x2 = x.reshape(b * s, d)
    for l in range(nl):
        x2 = _ffn(x2, l, p["ffn1_norm"], p["ffn1_w_gate"], p["ffn1_w_up"], p["ffn1_w_down"], fw, False)
        br = _branches(x2.reshape(b, s, d), cos3, sin3, l, p, ts)
        x2 = _merge(x2, br.reshape(b * s, N_BRANCH * BW), l, p["mix_norm"], p["w_gate"],
                    p["w_branch"], p["w_out"])
        x2 = _ffn(x2, l, p["ffn2_norm"], p["ffn2_w_gate"], p["ffn2_w_up"], p["ffn2_w_down"], fw,
                  l == nl - 1)
    return x2.reshape(b, s, d)


def kernel(x, positions, ffn1_norm, ffn1_w_gate, ffn1_w_up, ffn1_w_down, mix_norm, w_in, gdn_conv_w, gdn_a_log, gdn_dt_bias, gdn_norm_w, s5_lambda_re, s5_lambda_im, s5_b_re, s5_b_im, s5_c_re, s5_c_im, s5_d, s5_log_dt, s5_w_glu, s5_b_glu, lru_conv_w, lru_conv_b, lru_w_a, lru_b_a, lru_w_i, lru_b_i, lru_lambda, ret_norm_w, w_branch, w_out, ffn2_norm, ffn2_w_gate, ffn2_w_up, ffn2_w_down, final_norm):
    prm = dict(
        ffn1_norm=ffn1_norm, ffn1_w_gate=ffn1_w_gate, ffn1_w_up=ffn1_w_up, ffn1_w_down=ffn1_w_down,
        mix_norm=mix_norm, w_in=w_in, gdn_conv_w=gdn_conv_w, gdn_a_log=gdn_a_log,
        gdn_dt_bias=gdn_dt_bias, gdn_norm_w=gdn_norm_w, s5_lambda_re=s5_lambda_re,
        s5_lambda_im=s5_lambda_im, s5_b_re=s5_b_re, s5_b_im=s5_b_im, s5_c_re=s5_c_re, s5_c_im=s5_c_im,
        s5_d=s5_d, s5_log_dt=s5_log_dt, s5_w_glu=s5_w_glu, s5_b_glu=s5_b_glu, lru_conv_w=lru_conv_w,
        lru_conv_b=lru_conv_b, lru_w_a=lru_w_a, lru_b_a=lru_b_a, lru_w_i=lru_w_i, lru_b_i=lru_b_i,
        lru_lambda=lru_lambda, ret_norm_w=ret_norm_w, w_branch=w_branch, w_out=w_out,
        ffn2_norm=ffn2_norm, ffn2_w_gate=ffn2_w_gate, ffn2_w_up=ffn2_w_up, ffn2_w_down=ffn2_w_down)
    return _trunk(x, positions, prm, final_norm, SEQ_TILE)
```

```python
import functools
import math

import numpy as np
import jax
import jax.numpy as jnp
from jax import lax
from jax.experimental import pallas as pl
from jax.experimental.pallas import tpu as pltpu

F32 = jnp.float32
BF16 = jnp.bfloat16

D_MODEL = 1024
D_FF = 2816
NORM_EPS = 1e-6
N_BRANCH = 4
HEADS = 4
HEAD_DIM = 64
GDN_CONV = 4
GDN_CHUNK = 64
S5_GROUPS = 16
S5_GROUP_WIDTH = 16
S5_STATE = 64
S5_WIDTH = S5_GROUPS * S5_GROUP_WIDTH
S5_NSTATE = S5_GROUPS * S5_STATE
LRU_WIDTH = 256
LRU_BLOCKS = 4
LRU_CONV = 4
LRU_C = 8.0
ROPE_BASE = 10000.0
BW = 256

LANES = 128
SUB = 8
HALO = SUB

SEQ_TILE = 256
TOK_TILE = 512
FF_CHUNK = 256
VMEM_LIMIT = 56 * 1024 * 1024

C_QKV = 0
C_Z = C_QKV + 3 * BW
C_BETA = C_Z + BW
C_A = C_BETA + HEADS * LANES
C_S5 = C_A + HEADS * LANES
C_LX = C_S5 + BW
C_LY = C_LX + BW
C_RQ = C_LY + BW
C_RK = C_RQ + BW
C_RV = C_RK + BW
C_RG = C_RV + BW
N_A = C_RG + BW


def _sigmoid(x):
    return 1.0 / (1.0 + jnp.exp(-x))


def _silu(x):
    return x * _sigmoid(x)


def _gelu_tanh(x):
    return 0.5 * x * (1.0 + jnp.tanh(math.sqrt(2.0 / math.pi) * (x + 0.044715 * (x * x * x))))


def _softplus(x):
    return jnp.maximum(x, 0.0) + jnp.log(1.0 + jnp.exp(-jnp.abs(x)))


def _rms(x, w):
    return x * lax.rsqrt(jnp.mean(x * x, axis=-1, keepdims=True) + NORM_EPS) * w


def _dot(a, b):
    return jnp.dot(a.astype(BF16), b.astype(BF16), preferred_element_type=F32)


def _dot_nt(a, b):
    return lax.dot_general(a.astype(BF16), b.astype(BF16), (((1,), (1,)), ((), ())),
                           preferred_element_type=F32)


def _dot_tn(a, b):
    return lax.dot_general(a.astype(BF16), b.astype(BF16), (((0,), (0,)), ((), ())),
                           preferred_element_type=F32)


def _split_bf16(x, n):
    terms = []
    for _ in range(n - 1):
        t = x.astype(BF16)
        terms.append(t)
        x = x - t.astype(F32)
    terms.append(x.astype(BF16))
    return terms


def _dot_wide_rhs(m_bf16, x, n):
    return sum(jnp.dot(m_bf16, t, preferred_element_type=F32) for t in _split_bf16(x, n))


def _dot_wide_lhs(x, m_bf16, n):
    return sum(jnp.dot(t, m_bf16, preferred_element_type=F32) for t in _split_bf16(x, n))


def _iota2(shape, axis):
    return lax.broadcasted_iota(jnp.int32, shape, axis)


def _head_of(i):
    return lax.shift_right_logical(i, 6)


def _qhead_of(i):
    return lax.shift_right_logical(lax.bitwise_and(i, LANES - 1), 5)


def _ffn_kernel(x_ref, nw_ref, wg_ref, wu_ref, wd_ref, fw_ref, o_ref, *, final):
    x = x_ref[...]
    hn = _rms(x, nw_ref[...]).astype(BF16)
    acc = jnp.zeros(x.shape, F32)
    for c in range(D_FF // FF_CHUNK):
        cs = slice(c * FF_CHUNK, (c + 1) * FF_CHUNK)
        g = jnp.dot(hn, wg_ref[:, cs], preferred_element_type=F32)
        u = jnp.dot(hn, wu_ref[:, cs], preferred_element_type=F32)
        a = (_silu(g) * u).astype(BF16)
        acc = acc + jnp.dot(a, wd_ref[cs, :], preferred_element_type=F32)
    y = x + 0.5 * acc
    if final:
        y = _rms(y, fw_ref[...])
    o_ref[...] = y


def _ffn(x2, layer, nw, wg, wu, wd, fw, final):
    t = x2.shape[0]
    tm = min(TOK_TILE, t)
    wspec = lambda shape: pl.BlockSpec((None,) + shape, lambda i: (layer, 0, 0),
                                       pipeline_mode=pl.Buffered(1))
    return pl.pallas_call(
        functools.partial(_ffn_kernel, final=final),
        grid=(t // tm,),
        in_specs=[
            pl.BlockSpec((tm, D_MODEL), lambda i: (i, 0)),
            pl.BlockSpec((None, 1, D_MODEL), lambda i: (layer, 0, 0)),
            wspec((D_MODEL, D_FF)),
            wspec((D_MODEL, D_FF)),
            wspec((D_FF, D_MODEL)),
            pl.BlockSpec((1, D_MODEL), lambda i: (0, 0)),
        ],
        out_specs=pl.BlockSpec((tm, D_MODEL), lambda i: (i, 0)),
        out_shape=jax.ShapeDtypeStruct((t, D_MODEL), F32),
        compiler_params=pltpu.CompilerParams(
            dimension_semantics=("arbitrary",), vmem_limit_bytes=VMEM_LIMIT),
        name="ffn",
    )(x2, nw, wg, wu, wd, fw)


def _merge_kernel(x_ref, br_ref, nw_ref, wgate_ref, wbr_ref, wout_ref, o_ref):
    x = x_ref[...]
    hn = _rms(x, nw_ref[...]).astype(BF16)
    merged = jnp.zeros(x.shape, F32)
    for n in range(N_BRANCH):
        gl = jnp.dot(hn, wgate_ref[:, n * D_MODEL:(n + 1) * D_MODEL], preferred_element_type=F32)
        bp = jnp.dot(br_ref[:, n * BW:(n + 1) * BW], wbr_ref[n], preferred_element_type=F32)
        merged = merged + _sigmoid(gl) * bp
    o_ref[...] = x + jnp.dot(merged.astype(BF16), wout_ref[...], preferred_element_type=F32)


def _merge(x2, br, layer, nw, wgate, wbr, wout):
    t = x2.shape[0]
    tm = min(TOK_TILE, t)
    return pl.pallas_call(
        _merge_kernel,
        grid=(t // tm,),
        in_specs=[
            pl.BlockSpec((tm, D_MODEL), lambda i: (i, 0)),
            pl.BlockSpec((tm, N_BRANCH * BW), lambda i: (i, 0)),
            pl.BlockSpec((None, 1, D_MODEL), lambda i: (layer, 0, 0)),
            pl.BlockSpec((None, D_MODEL, N_BRANCH * D_MODEL), lambda i: (layer, 0, 0),
                         pipeline_mode=pl.Buffered(1)),
            pl.BlockSpec((None, N_BRANCH, BW, D_MODEL), lambda i: (layer, 0, 0, 0),
                         pipeline_mode=pl.Buffered(1)),
            pl.BlockSpec((None, D_MODEL, D_MODEL), lambda i: (layer, 0, 0),
                         pipeline_mode=pl.Buffered(1)),
        ],
        out_specs=pl.BlockSpec((tm, D_MODEL), lambda i: (i, 0)),
        out_shape=jax.ShapeDtypeStruct((t, D_MODEL), F32),
        compiler_params=pltpu.CompilerParams(
            dimension_semantics=("arbitrary",), vmem_limit_bytes=VMEM_LIMIT),
        name="merge",
    )(x2, br, nw, wgate, wbr, wout)


def _rope_kernel(pos_ref, freq_ref, cos_ref, sin_ref):
    ang = pos_ref[...] * freq_ref[...]
    cos_ref[...] = jnp.cos(ang)
    sin_ref[...] = jnp.sin(ang)


def _rope_tables(pos_f32):
    t = pos_f32.shape[0]
    rows = min(1024, t)
    half = HEAD_DIM // 2
    freq = np.power(ROPE_BASE, -np.arange(half, dtype=np.float32) / half).astype(np.float32)
    freq = jnp.asarray(np.tile(freq, LANES // half)[None, :])
    return pl.pallas_call(
        _rope_kernel,
        grid=(t // rows,),
        in_specs=[pl.BlockSpec((rows, 1), lambda i: (i, 0)),
                  pl.BlockSpec((1, LANES), lambda i: (0, 0))],
        out_specs=[pl.BlockSpec((rows, LANES), lambda i: (i, 0)),
                   pl.BlockSpec((rows, LANES), lambda i: (i, 0))],
        out_shape=[jax.ShapeDtypeStruct((t, LANES), F32)] * 2,
        name="rope_tables",
    )(pos_f32, freq)


def _s5_prep_kernel(lr_ref, li_ref, ldt_ref, bre_ref, bim_ref, cre_ref, cim_ref,
                    bbar_ref, cmat_ref, tab_ref, *, nsteps):
    lr, li = lr_ref[...], li_ref[...]
    dt = jnp.exp(ldt_ref[...])
    mag = jnp.exp(lr * dt)
    ar, ai = mag * jnp.cos(li * dt), mag * jnp.sin(li * dt)
    den = lr * lr + li * li
    nr = ar - 1.0
    cr = (nr * lr + ai * li) / den
    ci = (ai * lr - nr * li) / den
    bre, bim = bre_ref[...], bim_ref[...]
    bbar_ref[:, 0:S5_NSTATE] = (cr * bre - ci * bim).astype(BF16)
    bbar_ref[:, S5_NSTATE:] = (cr * bim + ci * bre).astype(BF16)
    cmat_ref[0:S5_NSTATE, :] = cre_ref[...].astype(BF16)
    cmat_ref[S5_NSTATE:, :] = (-cim_ref[...]).astype(BF16)
    def put(r, vr, vi):
        tab_ref[r:r + 1, 0:S5_NSTATE] = vr
        tab_ref[r:r + 1, S5_NSTATE:] = vi

    zero = jnp.zeros_like(ar)
    npow = _pow_rows(nsteps)
    pows = []
    pr, pi = ar, ai
    for k in range(npow):
        put(k, pr, pi)
        pows.append((pr, pi))
        pr, pi = pr * pr - pi * pi, 2.0 * pr * pi
    pr, pi = ar, ai
    for r in range(SUB):
        put(npow + r, pr, pi)
        pr, pi = pr * ar - pi * ai, pr * ai + pi * ar
    for k in range(3):
        for r in range(SUB):
            keep = r >= (1 << k)
            put(npow + SUB + k * SUB + r, pows[k][0] if keep else zero, pows[k][1] if keep else zero)


def _pow_rows(nsteps):
    return SUB * ((nsteps + SUB - 1) // SUB)


def _s5_prep(lr, li, ldt, bre, bim, cre, cim, nsteps):
    nl = lr.shape[0]
    nrows = _pow_rows(nsteps) + 4 * SUB
    vec = pl.BlockSpec((None, 1, S5_NSTATE), lambda l: (l, 0, 0))
    return pl.pallas_call(
        functools.partial(_s5_prep_kernel, nsteps=nsteps),
        grid=(nl,),
        in_specs=[vec, vec, vec,
                  pl.BlockSpec((None, S5_WIDTH, S5_NSTATE), lambda l: (l, 0, 0)),
                  pl.BlockSpec((None, S5_WIDTH, S5_NSTATE), lambda l: (l, 0, 0)),
                  pl.BlockSpec((None, S5_NSTATE, S5_WIDTH), lambda l: (l, 0, 0)),
                  pl.BlockSpec((None, S5_NSTATE, S5_WIDTH), lambda l: (l, 0, 0))],
        out_specs=[pl.BlockSpec((None, S5_WIDTH, 2 * S5_NSTATE), lambda l: (l, 0, 0)),
                   pl.BlockSpec((None, 2 * S5_NSTATE, S5_WIDTH), lambda l: (l, 0, 0)),
                   pl.BlockSpec((None, nrows, 2 * S5_NSTATE), lambda l: (l, 0, 0))],
        out_shape=[jax.ShapeDtypeStruct((nl, S5_WIDTH, 2 * S5_NSTATE), BF16),
                   jax.ShapeDtypeStruct((nl, 2 * S5_NSTATE, S5_WIDTH), BF16),
                   jax.ShapeDtypeStruct((nl, nrows, 2 * S5_NSTATE), F32)],
        name="s5_prep",
    )(lr, li, ldt, bre, bim, cre, cim)


def _branch_kernel(
        x_ref, cos_ref, sin_ref, nw_ref, win_ref,
        gconv_ref, galog_ref, gdtb_ref, gnorm_ref,
        bbar_ref, cmat_ref, tab_ref, s5d_ref, wglu_ref, bglu_ref,
        lcw_ref, lcb_ref, lwa_ref, lba_ref, lwi_ref, lbi_ref, llam_ref,
        rnorm_ref, rintra_ref, rxi_ref, rzeta_ref, rdec_ref,
        o_ref,
        gbuf, lbuf, qkv_s, gcb_s, og_s, xs_s, ge_s, gstate, s5c, lruc, rstate,
        *, ts, nsteps):
    first = pl.program_id(1) == 0

    @pl.when(first)
    def _():
        gbuf[0:HALO, :] = jnp.zeros((HALO, 3 * BW), F32)
        lbuf[0:HALO, :] = jnp.zeros((HALO, BW), F32)
        gstate[...] = jnp.zeros(gstate.shape, F32)
        rstate[...] = jnp.zeros(rstate.shape, F32)
        s5c[...] = jnp.zeros(s5c.shape, F32)
        lruc[...] = jnp.zeros(lruc.shape, F32)

    hn = _rms(x_ref[...], nw_ref[...]).astype(BF16)

    def proj(c0, width):
        return jnp.dot(hn, win_ref[:, c0:c0 + width], preferred_element_type=F32)

    r2 = _iota2((BW, BW), 0)
    c2 = _iota2((BW, BW), 1)
    same_head = _head_of(r2) == _head_of(c2)
    ones_bd = jnp.where(same_head, 1.0, 0.0).astype(BF16)
    lane = _iota2((1, BW), 1)

    def head_sum(v):
        return _dot_wide_lhs(v, ones_bd, 2)

    gbuf[HALO:HALO + ts, :] = proj(C_QKV, 3 * BW)
    conv = jnp.zeros((ts, 3 * BW), F32)
    for k in range(GDN_CONV):
        off = HALO - (GDN_CONV - 1) + k
        conv = conv + gconv_ref[k:k + 1, :] * gbuf[off:off + ts, :]
    gbuf[0:HALO, :] = gbuf[ts:ts + HALO, :]
    qkv = _silu(conv)
    q, k_, v = qkv[:, 0:BW], qkv[:, BW:2 * BW], qkv[:, 2 * BW:3 * BW]
    q = q * lax.rsqrt(head_sum(q * q) + 1e-6) * (HEAD_DIM ** -0.5)
    k_ = k_ * lax.rsqrt(head_sum(k_ * k_) + 1e-6)
    qkv_s[:, 0:BW] = q
    qkv_s[:, BW:2 * BW] = k_
    qkv_s[:, 2 * BW:3 * BW] = v

    beta = _sigmoid(proj(C_BETA, HEADS * LANES))
    g = -jnp.exp(galog_ref[...]) * _softplus(proj(C_A, HEADS * LANES) + gdtb_ref[...])
    rt = _iota2((ts, ts), 0)
    ct = _iota2((ts, ts), 1)
    mchunk = jnp.where((_head_of(rt) == _head_of(ct)) & (ct <= rt), 1.0, 0.0).astype(BF16)
    gcb_s[:, 0:HEADS * LANES] = _dot_wide_rhs(mchunk, g, 3)
    gcb_s[:, HEADS * LANES:] = beta

    hmask = [jnp.where(_head_of(lane) == h, 1.0, 0.0) for h in range(HEADS)]
    bd_b = jnp.where(same_head, 1.0, 0.0).astype(BF16)
    rc = _iota2((GDN_CHUNK, BW), 0)
    jc = lax.bitwise_and(_iota2((GDN_CHUNK, BW), 1), HEAD_DIM - 1)
    incl = jc <= rc
    strict = jc < rc
    eye = jnp.where(jc == rc, 1.0, 0.0)
    low_lane = _iota2((GDN_CHUNK, LANES), 1) < HEAD_DIM

    def stack(m):
        mb = m.astype(BF16)
        return jnp.concatenate([mb] * HEADS, axis=0) * bd_b

    def per_head_lanes(blocks):
        return jnp.concatenate([jnp.where(low_lane, blocks[0], blocks[1]),
                                jnp.where(low_lane, blocks[2], blocks[3])], axis=1)

    nchunk = ts // GDN_CHUNK
    crow = [slice(n * GDN_CHUNK, (n + 1) * GDN_CHUNK) for n in range(nchunk)]
    nt_dims = (((1,), (1,)), ((), ()))

    def gdn_local(n):
        qc = qkv_s[crow[n], 0:BW]
        kc = qkv_s[crow[n], BW:2 * BW]
        vc = qkv_s[crow[n], 2 * BW:3 * BW]
        gh = [gcb_s[crow[n], h * LANES:(h + 1) * LANES] for h in range(HEADS)]
        bh = [gcb_s[crow[n], (HEADS + h) * LANES:(HEADS + h + 1) * LANES] for h in range(HEADS)]
        g64 = per_head_lanes(gh)
        b64 = per_head_lanes(bh)
        grow = jnp.concatenate(gh, axis=0).T[0:GDN_CHUNK, :]
        decay = jnp.where(incl, jnp.exp(jnp.where(incl, g64 - grow, 0.0)), 0.0)
        ks_b = stack(kc)
        kk = lax.dot_general(kc.astype(BF16), ks_b, nt_dims, preferred_element_type=F32)
        qk = lax.dot_general(qc.astype(BF16), ks_b, nt_dims, preferred_element_type=F32)
        low = jnp.where(strict, b64 * kk * decay, 0.0)
        e64 = jnp.exp(g64)
        glast = g64[GDN_CHUNK - 1:GDN_CHUNK, :]
        return dict(
            tinv=eye - low, p=low, p_bd=stack(low),
            rhs_v=stack(vc * b64), rhs_k=stack(kc * (b64 * e64)), attn_b=(qk * decay).astype(BF16),
            q_dec=(qc * e64).astype(BF16), k_dec=(kc * jnp.exp(glast - g64)).astype(BF16),
            cdec=jnp.exp(glast))

    def gdn_inverse_step(c):
        c["p"] = jnp.dot(c["p"].astype(BF16), c["p_bd"], preferred_element_type=F32)
        c["p_bd"] = stack(c["p"])
        c["tinv"] = c["tinv"] + jnp.dot(c["tinv"].astype(BF16), c["p_bd"], preferred_element_type=F32)

    def gdn_solve(c):
        tinv_b = c["tinv"].astype(BF16)
        c["u"] = jnp.dot(tinv_b, c["rhs_v"], preferred_element_type=F32)
        c["w"] = jnp.dot(tinv_b, c["rhs_k"], preferred_element_type=F32).astype(BF16)

    def gdn_state_step(n, c):
        st = gstate[...]
        st_b = st.astype(BF16)
        v_new = c["u"] - jnp.dot(c["w"], st_b, preferred_element_type=F32)
        og_s[crow[n], :] = (jnp.dot(c["q_dec"], st_b, preferred_element_type=F32)
                            + jnp.dot(c["attn_b"], stack(v_new), preferred_element_type=F32))
        kv = lax.dot_general(c["k_dec"], v_new.astype(BF16), (((0,), (0,)), ((), ())),
                             preferred_element_type=F32)
        gstate[...] = st * c["cdec"] + jnp.where(same_head, kv, 0.0)

    u5 = proj(C_S5, BW)
    xs_s[...] = _dot(u5, bbar_ref[...])
    ngrp = ts // SUB
    npow = _pow_rows(nsteps)
    rowe = _iota2((ngrp, LANES), 0)

    def s5_slab(j):
        lr_ = slice(j * LANES, (j + 1) * LANES)
        li_ = slice(S5_NSTATE + j * LANES, S5_NSTATE + (j + 1) * LANES)
        xr = xs_s[:, lr_].reshape(ngrp, SUB, LANES)
        xi = xs_s[:, li_].reshape(ngrp, SUB, LANES)
        for s in range(3):
            m0 = npow + SUB + s * SUB
            mr, mi = tab_ref[m0:m0 + SUB, lr_][None], tab_ref[m0:m0 + SUB, li_][None]
            sr, si = pltpu.roll(xr, 1 << s, 1), pltpu.roll(xi, 1 << s, 1)
            xr, xi = xr + (mr * sr - mi * si), xi + (mr * si + mi * sr)
        ge_s[2 * j] = xr.reshape(ts, LANES)
        ge_s[2 * j + 1] = xi.reshape(ts, LANES)
        er = ge_s[2 * j, pl.ds(SUB - 1, ngrp, stride=SUB), :]
        ei = ge_s[2 * j + 1, pl.ds(SUB - 1, ngrp, stride=SUB), :]
        cr, ci = s5c[0:1, lr_], s5c[0:1, li_]
        a8r, a8i = tab_ref[3:4, lr_], tab_ref[3:4, li_]
        er = er + jnp.where(rowe == 0, a8r * cr - a8i * ci, 0.0)
        ei = ei + jnp.where(rowe == 0, a8r * ci + a8i * cr, 0.0)
        for s in range(3, nsteps):
            d = 1 << (s - 3)
            pr, pi = tab_ref[s:s + 1, lr_], tab_ref[s:s + 1, li_]
            sr = jnp.where(rowe >= d, pltpu.roll(er, d, 0), 0.0)
            si = jnp.where(rowe >= d, pltpu.roll(ei, d, 0), 0.0)
            er, ei = er + (pr * sr - pi * si), ei + (pr * si + pi * sr)
        s5c[0:1, lr_] = er[ngrp - 1:ngrp, :]
        s5c[0:1, li_] = ei[ngrp - 1:ngrp, :]
        pr_ = jnp.where(rowe == 0, cr, pltpu.roll(er, 1, 0))
        pi_ = jnp.where(rowe == 0, ci, pltpu.roll(ei, 1, 0))
        tr, ti = tab_ref[npow:npow + SUB, lr_], tab_ref[npow:npow + SUB, li_]
        for gi in range(ngrp):
            rs = slice(gi * SUB, (gi + 1) * SUB)
            cgr, cgi = pr_[gi:gi + 1, :], pi_[gi:gi + 1, :]
            xs_s[rs, lr_] = xr[gi] + (tr * cgr - ti * cgi)
            xs_s[rs, li_] = xi[gi] + (tr * cgi + ti * cgr)

    lbuf[HALO:HALO + ts, :] = proj(C_LX, BW)
    xc = jnp.zeros((ts, BW), F32) + lcb_ref[...]
    for k in range(LRU_CONV):
        off = HALO - (LRU_CONV - 1) + k
        xc = xc + lcw_ref[k:k + 1, :] * lbuf[off:off + ts, :]
    lbuf[0:HALO, :] = lbuf[ts:ts + HALO, :]
    rg = _sigmoid(_dot(xc, lwa_ref[...]) + lba_ref[...])
    ig = _sigmoid(_dot(xc, lwi_ref[...]) + lbi_ref[...])
    log_a = -LRU_C * rg * _softplus(-llam_ref[...])
    la = jnp.exp(log_a)
    mult = jnp.sqrt(1.0 - jnp.exp(2.0 * log_a))
    rowl = _iota2((ts, BW), 0)
    lru = dict(a=la, hx=xc * ig * mult + jnp.where(rowl == 0, la * lruc[0:1, :], 0.0))

    def lru_scan_step(s):
        d = 1 << s
        sa = jnp.where(rowl >= d, pltpu.roll(lru["a"], d, 0), 1.0)
        sx = jnp.where(rowl >= d, pltpu.roll(lru["hx"], d, 0), 0.0)
        lru["hx"] = lru["hx"] + lru["a"] * sx
        lru["a"] = lru["a"] * sa

    cos, sin = cos_ref[...], sin_ref[...]

    def rope(t):
        t1, t2 = t[:, 0:LANES], t[:, LANES:]
        return jnp.concatenate([t1 * cos - t2 * sin, t1 * sin + t2 * cos], axis=1)

    rq = rope(proj(C_RQ, BW))
    rk = rope(proj(C_RK, BW)) * (HEAD_DIM ** -0.5)
    rv = proj(C_RV, BW)
    rv_b = rv.astype(BF16)
    rk_b = rk.astype(BF16)
    qmask = [jnp.where(_qhead_of(lane) == h, 1.0, 0.0) for h in range(HEADS)]
    orr = _dot(rq * rxi_ref[...], rstate[...])
    for h in range(HEADS):
        sc = _dot_nt(rq * qmask[h], rk_b) * rintra_ref[h]
        orr = orr + hmask[h] * _dot(sc, rv_b)
    qv_same = _qhead_of(r2) == _head_of(c2)
    rstate[...] = rstate[...] * rdec_ref[...] + jnp.where(
        qv_same, _dot_tn(rk * rzeta_ref[...], rv_b), 0.0)
    mu = head_sum(orr) * (1.0 / HEAD_DIM)
    cen = orr - mu
    var = head_sum(cen * cen) * (1.0 / HEAD_DIM)
    on = cen * lax.rsqrt(var + NORM_EPS) * rnorm_ref[...]
    o_ref[:, 3 * BW:4 * BW] = (_silu(proj(C_RG, BW)) * on).astype(BF16)

    chunks = [gdn_local(n) for n in range(nchunk)]
    slabs = list(range(S5_NSTATE // LANES))
    for _ in range(5):
        for c in chunks:
            gdn_inverse_step(c)
        if slabs:
            s5_slab(slabs.pop(0))
    for c in chunks:
        gdn_solve(c)
        if slabs:
            s5_slab(slabs.pop(0))
    for j in slabs:
        s5_slab(j)
    per_chunk = -(-nsteps // nchunk)
    for n, c in enumerate(chunks):
        gdn_state_step(n, c)
        for s in range(n * per_chunk, min((n + 1) * per_chunk, nsteps)):
            lru_scan_step(s)

    og = og_s[...]
    og = og * lax.rsqrt(head_sum(og * og) * (1.0 / HEAD_DIM) + NORM_EPS) * gnorm_ref[...]
    o_ref[:, 0:BW] = (og * _silu(proj(C_Z, BW))).astype(BF16)

    y5 = _dot(xs_s[...], cmat_ref[...]) + s5d_ref[...] * u5
    y5 = _gelu_tanh(y5)
    y5 = y5 * _sigmoid(_dot(y5, wglu_ref[...]) + bglu_ref[...])
    o_ref[:, BW:2 * BW] = y5.astype(BF16)

    lruc[0:1, :] = lru["hx"][ts - 1:ts, :]
    o_ref[:, 2 * BW:3 * BW] = (lru["hx"] * _gelu_tanh(proj(C_LY, BW))).astype(BF16)


def _retention_tables(ts):
    hh = np.arange(HEADS, dtype=np.float64)
    log_gamma = np.log1p(-np.exp2(-5.0 - hh))
    idx = np.arange(ts, dtype=np.float64)
    rel = idx[:, None] - idx[None, :]
    intra = np.where(rel >= 0, np.exp(np.where(rel >= 0, rel, 0.0)[None] * log_gamma[:, None, None]), 0.0)
    xi = np.exp((idx + 1.0)[None] * log_gamma[:, None])
    zeta = np.exp((ts - 1.0 - idx)[None] * log_gamma[:, None])
    cdec = np.exp(ts * log_gamma)
    qhead = (np.arange(BW) % LANES) // (HEAD_DIM // 2)
    vhead = np.arange(BW) // HEAD_DIM
    xi_q = xi[qhead].T
    zeta_q = zeta[qhead].T
    dec = np.where(qhead[:, None] == vhead[None, :], cdec[qhead][:, None], 0.0)
    f = lambda a: jnp.asarray(a.astype(np.float32))
    return f(intra), f(xi_q), f(zeta_q), f(dec)


def _branches(x3, cos3, sin3, layer, p, ts):
    b, s, _ = x3.shape
    nsteps = int(math.log2(ts))
    assert 1 << nsteps == ts and s % ts == 0 and ts % GDN_CHUNK == 0
    rintra, rxi, rzeta, rdec = _retention_tables(ts)

    def lspec(shape, single=False):
        nd = len(shape)
        kw = {"pipeline_mode": pl.Buffered(1)} if single else {}
        return pl.BlockSpec((None,) + shape, lambda bi, si: (layer,) + (0,) * nd, **kw)

    def cspec(shape):
        nd = len(shape)
        return pl.BlockSpec(shape, lambda bi, si: (0,) * nd)

    tile = lambda w: pl.BlockSpec((None, ts, w), lambda bi, si: (bi, si, 0))
    in_specs = [
        tile(D_MODEL), tile(LANES), tile(LANES),
        lspec((1, D_MODEL)), lspec((D_MODEL, N_A), single=True),
        lspec((GDN_CONV, 3 * BW)), lspec((1, HEADS * LANES)), lspec((1, HEADS * LANES)), lspec((1, BW)),
        lspec((S5_WIDTH, 2 * S5_NSTATE)), lspec((2 * S5_NSTATE, S5_WIDTH)),
        lspec((p["s5_ap"].shape[1], 2 * S5_NSTATE)), lspec((1, BW)), lspec((BW, BW)), lspec((1, BW)),
        lspec((LRU_CONV, BW)), lspec((1, BW)), lspec((BW, BW)), lspec((1, BW)), lspec((BW, BW)),
        lspec((1, BW)), lspec((1, BW)),
        lspec((1, BW)), cspec((HEADS, ts, ts)), cspec((ts, BW)), cspec((ts, BW)), cspec((BW, BW)),
    ]
    scratch = [
        pltpu.VMEM((ts + HALO, 3 * BW), F32),
        pltpu.VMEM((ts + HALO, BW), F32),
        pltpu.VMEM((ts, 3 * BW), F32),
        pltpu.VMEM((ts, 2 * HEADS * LANES), F32),
        pltpu.VMEM((ts, BW), F32),
        pltpu.VMEM((ts, 2 * S5_NSTATE), F32),
        pltpu.VMEM((2 * S5_NSTATE // LANES, ts, LANES), F32),
        pltpu.VMEM((BW, BW), F32),
        pltpu.VMEM((HALO, 2 * S5_NSTATE), F32),
        pltpu.VMEM((HALO, BW), F32),
        pltpu.VMEM((BW, BW), F32),
    ]
    return pl.pallas_call(
        functools.partial(_branch_kernel, ts=ts, nsteps=nsteps),
        grid=(b, s // ts),
        in_specs=in_specs,
        out_specs=pl.BlockSpec((None, ts, N_BRANCH * BW), lambda bi, si: (bi, si, 0)),
        out_shape=jax.ShapeDtypeStruct((b, s, N_BRANCH * BW), BF16),
        scratch_shapes=scratch,
        compiler_params=pltpu.CompilerParams(
            dimension_semantics=("arbitrary", "arbitrary"), vmem_limit_bytes=VMEM_LIMIT),
        name="branches",
    )(x3, cos3, sin3, p["mix_norm"], p["w_in_a"],
      p["gdn_conv_w"], p["gdn_a_log"], p["gdn_dt_bias"], p["gdn_norm_w"],
      p["s5_bbar"], p["s5_cmat"], p["s5_ap"], p["s5_d"], p["s5_w_glu"], p["s5_b_glu"],
      p["lru_conv_w"], p["lru_conv_b"], p["lru_w_a"], p["lru_b_a"], p["lru_w_i"], p["lru_b_i"],
      p["lru_lambda"], p["ret_norm_w"], rintra, rxi, rzeta, rdec)


def _in_proj_columns():
    offs = np.cumsum([0, 3 * BW, BW, HEADS, HEADS, BW, BW, BW, BW, BW, BW, BW])
    qkv, z, beta, a, s5, lx, ly, rq, rk, rv, rg = offs[:11]
    p = np.arange(LANES)
    rope_perm = np.concatenate([(p // 32) * HEAD_DIM + p % 32, (p // 32) * HEAD_DIM + 32 + p % 32])
    cols = np.concatenate([
        qkv + np.arange(3 * BW), z + np.arange(BW),
        beta + np.repeat(np.arange(HEADS), LANES), a + np.repeat(np.arange(HEADS), LANES),
        s5 + np.arange(BW), lx + np.arange(BW), ly + np.arange(BW),
        rq + rope_perm, rk + rope_perm, rv + np.arange(BW), rg + np.arange(BW)])
    assert cols.shape[0] == N_A
    return cols, int(offs[11])


def _block_diag(blocks):
    nl, n, r, c = blocks.shape
    eye = jnp.eye(n, dtype=blocks.dtype)
    return (blocks[:, :, :, None, :] * eye[None, :, None, :, None]).reshape(nl, n * r, n * c)


def _prepare(prm, ts):
    nl = prm["w_in"].shape[0]
    cols, gate0 = _in_proj_columns()
    row = lambda a: a.reshape(nl, 1, -1)
    rep = lambda a, n: jnp.repeat(a, n, axis=-1).reshape(nl, 1, -1)
    p = {}
    p["w_in_a"] = jnp.take(prm["w_in"], jnp.asarray(cols), axis=2).astype(BF16)
    p["w_gate"] = prm["w_in"][:, :, gate0:].astype(BF16)
    p["mix_norm"] = row(prm["mix_norm"])
    p["gdn_conv_w"] = prm["gdn_conv_w"]
    p["gdn_a_log"] = rep(prm["gdn_a_log"], LANES)
    p["gdn_dt_bias"] = rep(prm["gdn_dt_bias"], LANES)
    p["gdn_norm_w"] = jnp.tile(prm["gdn_norm_w"], (1, HEADS)).reshape(nl, 1, BW)
    lr = row(prm["s5_lambda_re"])
    li = row(prm["s5_lambda_im"])
    ldt = rep(prm["s5_log_dt"], S5_STATE)
    bre = _block_diag(jnp.swapaxes(prm["s5_b_re"], 2, 3))
    bim = _block_diag(jnp.swapaxes(prm["s5_b_im"], 2, 3))
    cre = _block_diag(jnp.swapaxes(prm["s5_c_re"], 2, 3))
    cim = _block_diag(jnp.swapaxes(prm["s5_c_im"], 2, 3))
    p["s5_bbar"], p["s5_cmat"], p["s5_ap"] = _s5_prep(lr, li, ldt, bre, bim, cre, cim, int(math.log2(ts)))
    p["s5_d"] = row(prm["s5_d"])
    p["s5_w_glu"] = prm["s5_w_glu"].astype(BF16)
    p["s5_b_glu"] = row(prm["s5_b_glu"])
    p["lru_conv_w"] = prm["lru_conv_w"]
    p["lru_conv_b"] = row(prm["lru_conv_b"])
    p["lru_w_a"] = _block_diag(prm["lru_w_a"]).astype(BF16)
    p["lru_w_i"] = _block_diag(prm["lru_w_i"]).astype(BF16)
    p["lru_b_a"] = row(prm["lru_b_a"])
    p["lru_b_i"] = row(prm["lru_b_i"])
    p["lru_lambda"] = row(prm["lru_lambda"])
    p["ret_norm_w"] = row(prm["ret_norm_w"])
    p["w_branch"] = prm["w_branch"].astype(BF16)
    p["w_out"] = prm["w_out"].astype(BF16)
    for f in ("ffn1", "ffn2"):
        p[f + "_norm"] = row(prm[f + "_norm"])
        for w in ("w_gate", "w_up", "w_down"):
            p[f + "_" + w] = prm[f + "_" + w].astype(BF16)
    return p


def _trunk(x, positions, prm, final_norm, ts):
    b, s, d = x.shape
    nl = prm["w_in"].shape[0]
    p = _prepare(prm, ts)
    cos, sin = _rope_tables(positions.astype(F32).reshape(b * s, 1))
    cos3, sin3 = cos.reshape(b, s, LANES), sin.reshape(b, s, LANES)
    fw = final_norm.reshape(1, d)
    x2 = x.reshape(b * s, d)
    for l in range(nl):
        x2 = _ffn(x2, l, p["ffn1_norm"], p["ffn1_w_gate"], p["ffn1_w_up"], p["ffn1_w_down"], fw, False)
        br = _branches(x2.reshape(b, s, d), cos3, sin3, l, p, ts)
        x2 = _merge(x2, br.reshape(b * s, N_BRANCH * BW), l, p["mix_norm"], p["w_gate"],
                    p["w_branch"], p["w_out"])
        x2 = _ffn(x2, l, p["ffn2_norm"], p["ffn2_w_gate"], p["ffn2_w_up"], p["ffn2_w_down"], fw,
                  l == nl - 1)
    return x2.reshape(b, s, d)


def kernel(x, positions, ffn1_norm, ffn1_w_gate, ffn1_w_up, ffn1_w_down, mix_norm, w_in, gdn_conv_w, gdn_a_log, gdn_dt_bias, gdn_norm_w, s5_lambda_re, s5_lambda_im, s5_b_re, s5_b_im, s5_c_re, s5_c_im, s5_d, s5_log_dt, s5_w_glu, s5_b_glu, lru_conv_w, lru_conv_b, lru_w_a, lru_b_a, lru_w_i, lru_b_i, lru_lambda, ret_norm_w, w_branch, w_out, ffn2_norm, ffn2_w_gate, ffn2_w_up, ffn2_w_down, final_norm):
    prm = dict(
        ffn1_norm=ffn1_norm, ffn1_w_gate=ffn1_w_gate, ffn1_w_up=ffn1_w_up, ffn1_w_down=ffn1_w_down,
        mix_norm=mix_norm, w_in=w_in, gdn_conv_w=gdn_conv_w, gdn_a_log=gdn_a_log,
        gdn_dt_bias=gdn_dt_bias, gdn_norm_w=gdn_norm_w, s5_lambda_re=s5_lambda_re,
        s5_lambda_im=s5_lambda_im, s5_b_re=s5_b_re, s5_b_im=s5_b_im, s5_c_re=s5_c_re, s5_c_im=s5_c_im,
        s5_d=s5_d, s5_log_dt=s5_log_dt, s5_w_glu=s5_w_glu, s5_b_glu=s5_b_glu, lru_conv_w=lru_conv_w,
        lru_conv_b=lru_conv_b, lru_w_a=lru_w_a, lru_b_a=lru_b_a, lru_w_i=lru_w_i, lru_b_i=lru_b_i,
        lru_lambda=lru_lambda, ret_norm_w=ret_norm_w, w_branch=w_branch, w_out=w_out,
        ffn2_norm=ffn2_norm, ffn2_w_gate=ffn2_w_gate, ffn2_w_up=ffn2_w_up, ffn2_w_down=ffn2_w_down)
    return _trunk(x, positions, prm, final_norm, SEQ_TILE)
```

```python
import functools
import math

import numpy as np
import jax
import jax.numpy as jnp
from jax import lax
from jax.experimental import pallas as pl
from jax.experimental.pallas import tpu as pltpu

F32 = jnp.float32
BF16 = jnp.bfloat16

D_MODEL = 1024
D_FF = 2816
NORM_EPS = 1e-6
N_BRANCH = 4
HEADS = 4
HEAD_DIM = 64
GDN_CONV = 4
GDN_CHUNK = 64
S5_GROUPS = 16
S5_GROUP_WIDTH = 16
S5_STATE = 64
S5_WIDTH = S5_GROUPS * S5_GROUP_WIDTH
S5_NSTATE = S5_GROUPS * S5_STATE
LRU_WIDTH = 256
LRU_BLOCKS = 4
LRU_CONV = 4
LRU_C = 8.0
ROPE_BASE = 10000.0
BW = 256

LANES = 128
SUB = 8
HALO = SUB

SEQ_TILE = 256
TOK_TILE = 512
FF_CHUNK = 256
VMEM_LIMIT = 56 * 1024 * 1024

C_QKV = 0
C_Z = C_QKV + 3 * BW
C_BA = C_Z + BW
C_S5 = C_BA + LANES
C_LX = C_S5 + BW
C_LY = C_LX + BW
C_RQ = C_LY + BW
C_RK = C_RQ + BW
C_RV = C_RK + BW
C_RG = C_RV + BW
N_A = C_RG + BW


def _sigmoid(x):
    return 1.0 / (1.0 + jnp.exp(-x))


def _silu(x):
    return x * _sigmoid(x)


def _gelu_tanh(x):
    return 0.5 * x * (1.0 + jnp.tanh(math.sqrt(2.0 / math.pi) * (x + 0.044715 * (x * x * x))))


def _softplus(x):
    return jnp.maximum(x, 0.0) + jnp.log(1.0 + jnp.exp(-jnp.abs(x)))


def _rms(x, w):
    return x * lax.rsqrt(jnp.mean(x * x, axis=-1, keepdims=True) + NORM_EPS) * w


def _dot(a, b):
    return jnp.dot(a.astype(BF16), b.astype(BF16), preferred_element_type=F32)


def _dot_nt(a, b):
    return lax.dot_general(a.astype(BF16), b.astype(BF16), (((1,), (1,)), ((), ())),
                           preferred_element_type=F32)


def _dot_tn(a, b):
    return lax.dot_general(a.astype(BF16), b.astype(BF16), (((0,), (0,)), ((), ())),
                           preferred_element_type=F32)


def _split_bf16(x, n):
    terms = []
    for _ in range(n - 1):
        t = x.astype(BF16)
        terms.append(t)
        x = x - t.astype(F32)
    terms.append(x.astype(BF16))
    return terms


def _dot_wide_rhs(m_bf16, x, n):
    return sum(jnp.dot(m_bf16, t, preferred_element_type=F32) for t in _split_bf16(x, n))


def _dot_wide_lhs(x, m_bf16, n):
    return sum(jnp.dot(t, m_bf16, preferred_element_type=F32) for t in _split_bf16(x, n))


def _iota2(shape, axis):
    return lax.broadcasted_iota(jnp.int32, shape, axis)


def _head_of(i):
    return lax.shift_right_logical(i, 6)


def _qhead_of(i):
    return lax.shift_right_logical(lax.bitwise_and(i, LANES - 1), 5)


def _ffn_kernel(x_ref, nw_ref, wg_ref, wu_ref, wd_ref, fw_ref, o_ref, *, final):
    x = x_ref[...]
    hn = _rms(x, nw_ref[...]).astype(BF16)
    acc = jnp.zeros(x.shape, F32)
    for c in range(D_FF // FF_CHUNK):
        cs = slice(c * FF_CHUNK, (c + 1) * FF_CHUNK)
        g = jnp.dot(hn, wg_ref[:, cs], preferred_element_type=F32)
        u = jnp.dot(hn, wu_ref[:, cs], preferred_element_type=F32)
        a = (_silu(g) * u).astype(BF16)
        acc = acc + jnp.dot(a, wd_ref[cs, :], preferred_element_type=F32)
    y = x + 0.5 * acc
    if final:
        y = _rms(y, fw_ref[...])
    o_ref[...] = y


def _ffn(x2, layer, nw, wg, wu, wd, fw, final):
    t = x2.shape[0]
    tm = min(TOK_TILE, t)
    wspec = lambda shape: pl.BlockSpec((None,) + shape, lambda i: (layer, 0, 0),
                                       pipeline_mode=pl.Buffered(1))
    return pl.pallas_call(
        functools.partial(_ffn_kernel, final=final),
        grid=(t // tm,),
        in_specs=[
            pl.BlockSpec((tm, D_MODEL), lambda i: (i, 0)),
            pl.BlockSpec((None, 1, D_MODEL), lambda i: (layer, 0, 0)),
            wspec((D_MODEL, D_FF)),
            wspec((D_MODEL, D_FF)),
            wspec((D_FF, D_MODEL)),
            pl.BlockSpec((1, D_MODEL), lambda i: (0, 0)),
        ],
        out_specs=pl.BlockSpec((tm, D_MODEL), lambda i: (i, 0)),
        out_shape=jax.ShapeDtypeStruct((t, D_MODEL), F32),
        compiler_params=pltpu.CompilerParams(
            dimension_semantics=("arbitrary",), vmem_limit_bytes=VMEM_LIMIT),
        name="ffn",
    )(x2, nw, wg, wu, wd, fw)


def _merge_kernel(x_ref, br_ref, nw_ref, wgate_ref, wbr_ref, wout_ref, o_ref):
    x = x_ref[...]
    hn = _rms(x, nw_ref[...]).astype(BF16)
    merged = jnp.zeros(x.shape, F32)
    for n in range(N_BRANCH):
        gl = jnp.dot(hn, wgate_ref[:, n * D_MODEL:(n + 1) * D_MODEL], preferred_element_type=F32)
        bp = jnp.dot(br_ref[:, n * BW:(n + 1) * BW], wbr_ref[n], preferred_element_type=F32)
        merged = merged + _sigmoid(gl) * bp
    o_ref[...] = x + jnp.dot(merged.astype(BF16), wout_ref[...], preferred_element_type=F32)


def _merge(x2, br, layer, nw, wgate, wbr, wout):
    t = x2.shape[0]
    tm = min(TOK_TILE, t)
    return pl.pallas_call(
        _merge_kernel,
        grid=(t // tm,),
        in_specs=[
            pl.BlockSpec((tm, D_MODEL), lambda i: (i, 0)),
            pl.BlockSpec((tm, N_BRANCH * BW), lambda i: (i, 0)),
            pl.BlockSpec((None, 1, D_MODEL), lambda i: (layer, 0, 0)),
            pl.BlockSpec((None, D_MODEL, N_BRANCH * D_MODEL), lambda i: (layer, 0, 0),
                         pipeline_mode=pl.Buffered(1)),
            pl.BlockSpec((None, N_BRANCH, BW, D_MODEL), lambda i: (layer, 0, 0, 0),
                         pipeline_mode=pl.Buffered(1)),
            pl.BlockSpec((None, D_MODEL, D_MODEL), lambda i: (layer, 0, 0),
                         pipeline_mode=pl.Buffered(1)),
        ],
        out_specs=pl.BlockSpec((tm, D_MODEL), lambda i: (i, 0)),
        out_shape=jax.ShapeDtypeStruct((t, D_MODEL), F32),
        compiler_params=pltpu.CompilerParams(
            dimension_semantics=("arbitrary",), vmem_limit_bytes=VMEM_LIMIT),
        name="merge",
    )(x2, br, nw, wgate, wbr, wout)


def _rope_kernel(pos_ref, freq_ref, cos_ref, sin_ref):
    ang = pos_ref[...] * freq_ref[...]
    cos_ref[...] = jnp.cos(ang)
    sin_ref[...] = jnp.sin(ang)


def _rope_tables(pos_f32):
    t = pos_f32.shape[0]
    rows = min(1024, t)
    half = HEAD_DIM // 2
    freq = np.power(ROPE_BASE, -np.arange(half, dtype=np.float32) / half).astype(np.float32)
    freq = jnp.asarray(np.tile(freq, LANES // half)[None, :])
    return pl.pallas_call(
        _rope_kernel,
        grid=(t // rows,),
        in_specs=[pl.BlockSpec((rows, 1), lambda i: (i, 0)),
                  pl.BlockSpec((1, LANES), lambda i: (0, 0))],
        out_specs=[pl.BlockSpec((rows, LANES), lambda i: (i, 0)),
                   pl.BlockSpec((rows, LANES), lambda i: (i, 0))],
        out_shape=[jax.ShapeDtypeStruct((t, LANES), F32)] * 2,
        name="rope_tables",
    )(pos_f32, freq)


def _s5_prep_kernel(lr_ref, li_ref, ldt_ref, bre_ref, bim_ref, cre_ref, cim_ref,
                    bbar_ref, cmat_ref, tab_ref, *, nsteps):
    lr, li = lr_ref[...], li_ref[...]
    dt = jnp.exp(ldt_ref[...])
    mag = jnp.exp(lr * dt)
    ar, ai = mag * jnp.cos(li * dt), mag * jnp.sin(li * dt)
    den = lr * lr + li * li
    nr = ar - 1.0
    cr = (nr * lr + ai * li) / den
    ci = (ai * lr - nr * li) / den
    bre, bim = bre_ref[...], bim_ref[...]
    bbar_ref[:, 0:S5_NSTATE] = (cr * bre - ci * bim).astype(BF16)
    bbar_ref[:, S5_NSTATE:] = (cr * bim + ci * bre).astype(BF16)
    cmat_ref[0:S5_NSTATE, :] = cre_ref[...].astype(BF16)
    cmat_ref[S5_NSTATE:, :] = (-cim_ref[...]).astype(BF16)
    def put(r, vr, vi):
        tab_ref[r:r + 1, 0:S5_NSTATE] = vr
        tab_ref[r:r + 1, S5_NSTATE:] = vi

    zero = jnp.zeros_like(ar)
    npow = _pow_rows(nsteps)
    pows = []
    pr, pi = ar, ai
    for k in range(npow):
        put(k, pr, pi)
        pows.append((pr, pi))
        pr, pi = pr * pr - pi * pi, 2.0 * pr * pi
    pr, pi = ar, ai
    for r in range(SUB):
        put(npow + r, pr, pi)
        pr, pi = pr * ar - pi * ai, pr * ai + pi * ar
    for k in range(3):
        for r in range(SUB):
            keep = r >= (1 << k)
            put(npow + SUB + k * SUB + r, pows[k][0] if keep else zero, pows[k][1] if keep else zero)


def _pow_rows(nsteps):
    return SUB * ((nsteps + SUB - 1) // SUB)


def _s5_prep(lr, li, ldt, bre, bim, cre, cim, nsteps):
    nl = lr.shape[0]
    nrows = _pow_rows(nsteps) + 4 * SUB
    vec = pl.BlockSpec((None, 1, S5_NSTATE), lambda l: (l, 0, 0))
    return pl.pallas_call(
        functools.partial(_s5_prep_kernel, nsteps=nsteps),
        grid=(nl,),
        in_specs=[vec, vec, vec,
                  pl.BlockSpec((None, S5_WIDTH, S5_NSTATE), lambda l: (l, 0, 0)),
                  pl.BlockSpec((None, S5_WIDTH, S5_NSTATE), lambda l: (l, 0, 0)),
                  pl.BlockSpec((None, S5_NSTATE, S5_WIDTH), lambda l: (l, 0, 0)),
                  pl.BlockSpec((None, S5_NSTATE, S5_WIDTH), lambda l: (l, 0, 0))],
        out_specs=[pl.BlockSpec((None, S5_WIDTH, 2 * S5_NSTATE), lambda l: (l, 0, 0)),
                   pl.BlockSpec((None, 2 * S5_NSTATE, S5_WIDTH), lambda l: (l, 0, 0)),
                   pl.BlockSpec((None, nrows, 2 * S5_NSTATE), lambda l: (l, 0, 0))],
        out_shape=[jax.ShapeDtypeStruct((nl, S5_WIDTH, 2 * S5_NSTATE), BF16),
                   jax.ShapeDtypeStruct((nl, 2 * S5_NSTATE, S5_WIDTH), BF16),
                   jax.ShapeDtypeStruct((nl, nrows, 2 * S5_NSTATE), F32)],
        name="s5_prep",
    )(lr, li, ldt, bre, bim, cre, cim)


def _branch_kernel(
        x_ref, cos_ref, sin_ref, nw_ref, win_ref,
        gconv_ref, galog_ref, gdtb_ref, gnorm_ref,
        bbar_ref, cmat_ref, tab_ref, s5d_ref, wglu_ref, bglu_ref,
        lcw_ref, lcb_ref, lwa_ref, lba_ref, lwi_ref, lbi_ref, llam_ref,
        rnorm_ref, rintra_ref, rxi_ref, rzeta_ref, rdec_ref,
        o_ref,
        gbuf, lbuf, qkv_s, gcb_s, og_s, xs_s, ge_s, lg_s, gstate, s5c, lruc, rstate,
        *, ts, nsteps):
    first = pl.program_id(1) == 0

    @pl.when(first)
    def _():
        gbuf[...] = jnp.zeros(gbuf.shape, F32)
        lbuf[...] = jnp.zeros(lbuf.shape, F32)
        gstate[...] = jnp.zeros(gstate.shape, F32)
        rstate[...] = jnp.zeros(rstate.shape, F32)
        s5c[...] = jnp.zeros(s5c.shape, F32)
        lruc[...] = jnp.zeros(lruc.shape, F32)

    hn = _rms(x_ref[...], nw_ref[...]).astype(BF16)

    def proj(c0, width):
        return jnp.dot(hn, win_ref[:, c0:c0 + width], preferred_element_type=F32)

    r2 = _iota2((BW, BW), 0)
    c2 = _iota2((BW, BW), 1)
    same_head = _head_of(r2) == _head_of(c2)
    ones_bd = jnp.where(same_head, 1.0, 0.0).astype(BF16)
    lane = _iota2((1, BW), 1)

    def head_sum(v, terms=1):
        return _dot_wide_lhs(v, ones_bd, terms)

    def causal_conv(cur, hist_ref, w_ref, taps):
        ext = jnp.concatenate([hist_ref[...], cur], axis=0)
        hist_ref[...] = cur[ts - HALO:ts, :]
        y = w_ref[taps - 1:taps, :] * cur
        for d in range(1, taps):
            y = y + w_ref[taps - 1 - d:taps - d, :] * pltpu.roll(ext, d, 0)[HALO:, :]
        return y

    qkv = _silu(causal_conv(proj(C_QKV, 3 * BW), gbuf, gconv_ref, GDN_CONV))
    q, k_, v = qkv[:, 0:BW], qkv[:, BW:2 * BW], qkv[:, 2 * BW:3 * BW]
    q = q * lax.rsqrt(head_sum(q * q) + 1e-6) * (HEAD_DIM ** -0.5)
    k_ = k_ * lax.rsqrt(head_sum(k_ * k_) + 1e-6)
    qkv_s[:, 0:BW] = q
    qkv_s[:, BW:2 * BW] = k_
    qkv_s[:, 2 * BW:3 * BW] = v

    ba = proj(C_BA, LANES)
    beta = _sigmoid(ba)
    g = -jnp.exp(galog_ref[...]) * _softplus(ba + gdtb_ref[...])
    rt = _iota2((ts, ts), 0)
    ct = _iota2((ts, ts), 1)
    mchunk = jnp.where((_head_of(rt) == _head_of(ct)) & (ct <= rt), 1.0, 0.0).astype(BF16)
    gc = _dot_wide_rhs(mchunk, g, 3)
    for h in range(HEADS):
        gcb_s[:, h * LANES:(h + 1) * LANES] = jnp.broadcast_to(gc[:, HEADS + h:HEADS + h + 1], (ts, LANES))
        gcb_s[:, (HEADS + h) * LANES:(HEADS + h + 1) * LANES] = jnp.broadcast_to(beta[:, h:h + 1], (ts, LANES))

    hmask = [jnp.where(_head_of(lane) == h, 1.0, 0.0) for h in range(HEADS)]
    bd_b = jnp.where(same_head, 1.0, 0.0).astype(BF16)
    rc = _iota2((GDN_CHUNK, BW), 0)
    jc = lax.bitwise_and(_iota2((GDN_CHUNK, BW), 1), HEAD_DIM - 1)
    incl = jc <= rc
    strict = jc < rc
    eye = jnp.where(jc == rc, 1.0, 0.0)
    low_lane = _iota2((GDN_CHUNK, LANES), 1) < HEAD_DIM

    def stack(m):
        mb = m.astype(BF16)
        return jnp.concatenate([mb] * HEADS, axis=0) * bd_b

    def per_head_lanes(blocks):
        return jnp.concatenate([jnp.where(low_lane, blocks[0], blocks[1]),
                                jnp.where(low_lane, blocks[2], blocks[3])], axis=1)

    nchunk = ts // GDN_CHUNK
    crow = [slice(n * GDN_CHUNK, (n + 1) * GDN_CHUNK) for n in range(nchunk)]
    nt_dims = (((1,), (1,)), ((), ()))

    def gdn_local(n):
        qc = qkv_s[crow[n], 0:BW]
        kc = qkv_s[crow[n], BW:2 * BW]
        vc = qkv_s[crow[n], 2 * BW:3 * BW]
        gh = [gcb_s[crow[n], h * LANES:(h + 1) * LANES] for h in range(HEADS)]
        bh = [gcb_s[crow[n], (HEADS + h) * LANES:(HEADS + h + 1) * LANES] for h in range(HEADS)]
        g64 = per_head_lanes(gh)
        b64 = per_head_lanes(bh)
        grow = jnp.concatenate(gh, axis=0).T[0:GDN_CHUNK, :]
        decay = jnp.where(incl, jnp.exp(jnp.where(incl, g64 - grow, 0.0)), 0.0)
        ks_b = stack(kc)
        kk = lax.dot_general(kc.astype(BF16), ks_b, nt_dims, preferred_element_type=F32)
        qk = lax.dot_general(qc.astype(BF16), ks_b, nt_dims, preferred_element_type=F32)
        low = jnp.where(strict, b64 * kk * decay, 0.0)
        e64 = jnp.exp(g64)
        glast = g64[GDN_CHUNK - 1:GDN_CHUNK, :]
        return dict(
            tinv=eye - low, p=low, p_bd=stack(low),
            rhs_v=stack(vc * b64), rhs_k=stack(kc * (b64 * e64)), attn_b=(qk * decay).astype(BF16),
            q_dec=(qc * e64).astype(BF16), k_dec=(kc * jnp.exp(glast - g64)).astype(BF16),
            cdec=jnp.exp(glast))

    def gdn_inverse_step(c):
        c["p"] = jnp.dot(c["p"].astype(BF16), c["p_bd"], preferred_element_type=F32)
        c["p_bd"] = stack(c["p"])
        c["tinv"] = c["tinv"] + jnp.dot(c["tinv"].astype(BF16), c["p_bd"], preferred_element_type=F32)

    def gdn_solve(c):
        tinv_b = c["tinv"].astype(BF16)
        c["u"] = jnp.dot(tinv_b, c["rhs_v"], preferred_element_type=F32)
        c["w"] = jnp.dot(tinv_b, c["rhs_k"], preferred_element_type=F32).astype(BF16)

    def gdn_state_step(n, c):
        st = gstate[...]
        st_b = st.astype(BF16)
        v_new = c["u"] - jnp.dot(c["w"], st_b, preferred_element_type=F32)
        og_s[crow[n], :] = (jnp.dot(c["q_dec"], st_b, preferred_element_type=F32)
                            + jnp.dot(c["attn_b"], stack(v_new), preferred_element_type=F32))
        kv = lax.dot_general(c["k_dec"], v_new.astype(BF16), (((0,), (0,)), ((), ())),
                             preferred_element_type=F32)
        gstate[...] = st * c["cdec"] + jnp.where(same_head, kv, 0.0)

    u5 = proj(C_S5, BW)
    xs_s[...] = _dot(u5, bbar_ref[...])
    ngrp = ts // SUB
    npow = _pow_rows(nsteps)
    rowe = _iota2((ngrp, LANES), 0)

    def s5_slab(j):
        lr_ = slice(j * LANES, (j + 1) * LANES)
        li_ = slice(S5_NSTATE + j * LANES, S5_NSTATE + (j + 1) * LANES)
        xr = xs_s[:, lr_].reshape(ngrp, SUB, LANES)
        xi = xs_s[:, li_].reshape(ngrp, SUB, LANES)
        for s in range(3):
            m0 = npow + SUB + s * SUB
            mr, mi = tab_ref[m0:m0 + SUB, lr_][None], tab_ref[m0:m0 + SUB, li_][None]
            sr, si = pltpu.roll(xr, 1 << s, 1), pltpu.roll(xi, 1 << s, 1)
            xr, xi = xr + (mr * sr - mi * si), xi + (mr * si + mi * sr)
        ge_s[2 * j] = xr.reshape(ts, LANES)
        ge_s[2 * j + 1] = xi.reshape(ts, LANES)
        er = ge_s[2 * j, pl.ds(SUB - 1, ngrp, stride=SUB), :]
        ei = ge_s[2 * j + 1, pl.ds(SUB - 1, ngrp, stride=SUB), :]
        cr, ci = s5c[0:1, lr_], s5c[0:1, li_]
        a8r, a8i = tab_ref[3:4, lr_], tab_ref[3:4, li_]
        er = er + jnp.where(rowe == 0, a8r * cr - a8i * ci, 0.0)
        ei = ei + jnp.where(rowe == 0, a8r * ci + a8i * cr, 0.0)
        for s in range(3, nsteps):
            d = 1 << (s - 3)
            pr, pi = tab_ref[s:s + 1, lr_], tab_ref[s:s + 1, li_]
            sr = jnp.where(rowe >= d, pltpu.roll(er, d, 0), 0.0)
            si = jnp.where(rowe >= d, pltpu.roll(ei, d, 0), 0.0)
            er, ei = er + (pr * sr - pi * si), ei + (pr * si + pi * sr)
        s5c[0:1, lr_] = er[ngrp - 1:ngrp, :]
        s5c[0:1, li_] = ei[ngrp - 1:ngrp, :]
        pr_ = jnp.where(rowe == 0, cr, pltpu.roll(er, 1, 0))
        pi_ = jnp.where(rowe == 0, ci, pltpu.roll(ei, 1, 0))
        tr, ti = tab_ref[npow:npow + SUB, lr_], tab_ref[npow:npow + SUB, li_]
        for gi in range(ngrp):
            rs = slice(gi * SUB, (gi + 1) * SUB)
            cgr, cgi = pr_[gi:gi + 1, :], pi_[gi:gi + 1, :]
            xs_s[rs, lr_] = xr[gi] + (tr * cgr - ti * cgi)
            xs_s[rs, li_] = xi[gi] + (tr * cgi + ti * cgr)

    xc = causal_conv(proj(C_LX, BW), lbuf, lcw_ref, LRU_CONV) + lcb_ref[...]
    rg = _sigmoid(_dot(xc, lwa_ref[...]) + lba_ref[...])
    ig = _sigmoid(_dot(xc, lwi_ref[...]) + lbi_ref[...])
    log_a = -LRU_C * rg * _softplus(-llam_ref[...])
    la = jnp.exp(log_a)
    mult = jnp.sqrt(1.0 - jnp.exp(2.0 * log_a))
    rsub = lax.broadcasted_iota(jnp.int32, (ngrp, SUB, BW), 1)
    rowg = _iota2((ngrp, BW), 0)
    lru = dict(a=la.reshape(ngrp, SUB, BW), x=(xc * ig * mult).reshape(ngrp, SUB, BW))

    def lru_group_step(s):
        d = 1 << s
        sa = jnp.where(rsub >= d, pltpu.roll(lru["a"], d, 1), 1.0)
        sx = jnp.where(rsub >= d, pltpu.roll(lru["x"], d, 1), 0.0)
        lru["x"] = lru["x"] + lru["a"] * sx
        lru["a"] = lru["a"] * sa

    def lru_finish():
        a2, x2 = lru["a"].reshape(ts, BW), lru["x"].reshape(ts, BW)
        halves = BW // LANES
        for c in range(halves):
            lg_s[c] = a2[:, c * LANES:(c + 1) * LANES]
            lg_s[halves + c] = x2[:, c * LANES:(c + 1) * LANES]
        ends = pl.ds(SUB - 1, ngrp, stride=SUB)
        ea = jnp.concatenate([lg_s[c, ends, :] for c in range(halves)], axis=1)
        ex = jnp.concatenate([lg_s[halves + c, ends, :] for c in range(halves)], axis=1)
        carry = lruc[0:1, :]
        ex = ex + jnp.where(rowg == 0, ea * carry, 0.0)
        for s in range(nsteps - 3):
            d = 1 << s
            sa = jnp.where(rowg >= d, pltpu.roll(ea, d, 0), 1.0)
            sx = jnp.where(rowg >= d, pltpu.roll(ex, d, 0), 0.0)
            ex = ex + ea * sx
            ea = ea * sa
        lruc[0:1, :] = ex[ngrp - 1:ngrp, :]
        enter = jnp.where(rowg == 0, carry, pltpu.roll(ex, 1, 0))
        lru["hx"] = jnp.concatenate(
            [lru["x"][gi] + lru["a"][gi] * enter[gi:gi + 1, :] for gi in range(ngrp)], axis=0)

    cos, sin = cos_ref[...], sin_ref[...]

    def rope(t):
        t1, t2 = t[:, 0:LANES], t[:, LANES:]
        return jnp.concatenate([t1 * cos - t2 * sin, t1 * sin + t2 * cos], axis=1)

    rq = rope(proj(C_RQ, BW))
    rk = rope(proj(C_RK, BW)) * (HEAD_DIM ** -0.5)
    rv = proj(C_RV, BW)
    rv_b = rv.astype(BF16)
    rk_b = rk.astype(BF16)
    qmask = [jnp.where(_qhead_of(lane) == h, 1.0, 0.0) for h in range(HEADS)]
    orr = _dot(rq * rxi_ref[...], rstate[...])
    for h in range(HEADS):
        sc = _dot_nt(rq * qmask[h], rk_b) * rintra_ref[h]
        orr = orr + hmask[h] * _dot(sc, rv_b)
    qv_same = _qhead_of(r2) == _head_of(c2)
    rstate[...] = rstate[...] * rdec_ref[...] + jnp.where(
        qv_same, _dot_tn(rk * rzeta_ref[...], rv_b), 0.0)
    mu = head_sum(orr, 2) * (1.0 / HEAD_DIM)
    cen = orr - mu
    var = head_sum(cen * cen) * (1.0 / HEAD_DIM)
    on = cen * lax.rsqrt(var + NORM_EPS) * rnorm_ref[...]
    o_ref[:, 3 * BW:4 * BW] = (_silu(proj(C_RG, BW)) * on).astype(BF16)

    chunks = [gdn_local(n) for n in range(nchunk)]
    slabs = list(range(S5_NSTATE // LANES))
    for _ in range(5):
        for c in chunks:
            gdn_inverse_step(c)
        if slabs:
            s5_slab(slabs.pop(0))
    for c in chunks:
        gdn_solve(c)
        if slabs:
            s5_slab(slabs.pop(0))
    for j in slabs:
        s5_slab(j)
    lru_work = [functools.partial(lru_group_step, s) for s in range(3)] + [lru_finish]
    for n, c in enumerate(chunks):
        gdn_state_step(n, c)
        if lru_work:
            lru_work.pop(0)()
    for work in lru_work:
        work()

    og = og_s[...]
    og = og * lax.rsqrt(head_sum(og * og) * (1.0 / HEAD_DIM) + NORM_EPS) * gnorm_ref[...]
    o_ref[:, 0:BW] = (og * _silu(proj(C_Z, BW))).astype(BF16)

    y5 = _dot(xs_s[...], cmat_ref[...]) + s5d_ref[...] * u5
    y5 = _gelu_tanh(y5)
    y5 = y5 * _sigmoid(_dot(y5, wglu_ref[...]) + bglu_ref[...])
    o_ref[:, BW:2 * BW] = y5.astype(BF16)

    o_ref[:, 2 * BW:3 * BW] = (lru["hx"] * _gelu_tanh(proj(C_LY, BW))).astype(BF16)


def _retention_tables(ts):
    hh = np.arange(HEADS, dtype=np.float64)
    log_gamma = np.log1p(-np.exp2(-5.0 - hh))
    idx = np.arange(ts, dtype=np.float64)
    rel = idx[:, None] - idx[None, :]
    intra = np.where(rel >= 0, np.exp(np.where(rel >= 0, rel, 0.0)[None] * log_gamma[:, None, None]), 0.0)
    xi = np.exp((idx + 1.0)[None] * log_gamma[:, None])
    zeta = np.exp((ts - 1.0 - idx)[None] * log_gamma[:, None])
    cdec = np.exp(ts * log_gamma)
    qhead = (np.arange(BW) % LANES) // (HEAD_DIM // 2)
    vhead = np.arange(BW) // HEAD_DIM
    xi_q = xi[qhead].T
    zeta_q = zeta[qhead].T
    dec = np.where(qhead[:, None] == vhead[None, :], cdec[qhead][:, None], 0.0)
    f = lambda a: jnp.asarray(a.astype(np.float32))
    return f(intra), f(xi_q), f(zeta_q), f(dec)


def _branches(x3, cos3, sin3, layer, p, ts):
    b, s, _ = x3.shape
    nsteps = int(math.log2(ts))
    assert 1 << nsteps == ts and s % ts == 0 and ts % GDN_CHUNK == 0
    rintra, rxi, rzeta, rdec = _retention_tables(ts)

    def lspec(shape, single=False):
        nd = len(shape)
        kw = {"pipeline_mode": pl.Buffered(1)} if single else {}
        return pl.BlockSpec((None,) + shape, lambda bi, si: (layer,) + (0,) * nd, **kw)

    def cspec(shape):
        nd = len(shape)
        return pl.BlockSpec(shape, lambda bi, si: (0,) * nd)

    tile = lambda w: pl.BlockSpec((None, ts, w), lambda bi, si: (bi, si, 0))
    in_specs = [
        tile(D_MODEL), tile(LANES), tile(LANES),
        lspec((1, D_MODEL)), lspec((D_MODEL, N_A), single=True),
        lspec((GDN_CONV, 3 * BW)), lspec((1, LANES)), lspec((1, LANES)), lspec((1, BW)),
        lspec((S5_WIDTH, 2 * S5_NSTATE)), lspec((2 * S5_NSTATE, S5_WIDTH)),
        lspec((p["s5_ap"].shape[1], 2 * S5_NSTATE)), lspec((1, BW)), lspec((BW, BW)), lspec((1, BW)),
        lspec((LRU_CONV, BW)), lspec((1, BW)), lspec((BW, BW)), lspec((1, BW)), lspec((BW, BW)),
        lspec((1, BW)), lspec((1, BW)),
        lspec((1, BW)), cspec((HEADS, ts, ts)), cspec((ts, BW)), cspec((ts, BW)), cspec((BW, BW)),
    ]
    scratch = [
        pltpu.VMEM((HALO, 3 * BW), F32),
        pltpu.VMEM((HALO, BW), F32),
        pltpu.VMEM((ts, 3 * BW), F32),
        pltpu.VMEM((ts, 2 * HEADS * LANES), F32),
        pltpu.VMEM((ts, BW), F32),
        pltpu.VMEM((ts, 2 * S5_NSTATE), F32),
        pltpu.VMEM((2 * S5_NSTATE // LANES, ts, LANES), F32),
        pltpu.VMEM((2 * BW // LANES, ts, LANES), F32),
        pltpu.VMEM((BW, BW), F32),
        pltpu.VMEM((HALO, 2 * S5_NSTATE), F32),
        pltpu.VMEM((HALO, BW), F32),
        pltpu.VMEM((BW, BW), F32),
    ]
    return pl.pallas_call(
        functools.partial(_branch_kernel, ts=ts, nsteps=nsteps),
        grid=(b, s // ts),
        in_specs=in_specs,
        out_specs=pl.BlockSpec((None, ts, N_BRANCH * BW), lambda bi, si: (bi, si, 0)),
        out_shape=jax.ShapeDtypeStruct((b, s, N_BRANCH * BW), BF16),
        scratch_shapes=scratch,
        compiler_params=pltpu.CompilerParams(
            dimension_semantics=("arbitrary", "arbitrary"), vmem_limit_bytes=VMEM_LIMIT),
        name="branches",
    )(x3, cos3, sin3, p["mix_norm"], p["w_in_a"],
      p["gdn_conv_w"], p["gdn_a_log"], p["gdn_dt_bias"], p["gdn_norm_w"],
      p["s5_bbar"], p["s5_cmat"], p["s5_ap"], p["s5_d"], p["s5_w_glu"], p["s5_b_glu"],
      p["lru_conv_w"], p["lru_conv_b"], p["lru_w_a"], p["lru_b_a"], p["lru_w_i"], p["lru_b_i"],
      p["lru_lambda"], p["ret_norm_w"], rintra, rxi, rzeta, rdec)


def _split_in_proj(w_in):
    offs = np.cumsum([0, 3 * BW, BW, HEADS, HEADS, BW, BW, BW, BW, BW, BW, BW])
    qkv, _, beta, _, s5, _, _, rq, rk, rv, _, gate0 = (int(o) for o in offs)
    nl, d, _ = w_in.shape

    def rope_split(w):
        return w.reshape(nl, d, HEADS, 2, HEAD_DIM // 2).transpose(0, 1, 3, 2, 4).reshape(nl, d, BW)

    ba = jnp.pad(w_in[:, :, beta:s5], ((0, 0), (0, 0), (0, LANES - 2 * HEADS)))
    w_a = jnp.concatenate([w_in[:, :, qkv:beta], ba, w_in[:, :, s5:rq], rope_split(w_in[:, :, rq:rk]),
                           rope_split(w_in[:, :, rk:rv]), w_in[:, :, rv:gate0]], axis=2)
    assert w_a.shape[2] == N_A
    return w_a, w_in[:, :, gate0:]


def _block_diag(blocks):
    nl, n, r, c = blocks.shape
    eye = jnp.eye(n, dtype=blocks.dtype)
    return (blocks[:, :, :, None, :] * eye[None, :, None, :, None]).reshape(nl, n * r, n * c)


def _prepare(prm, ts):
    nl = prm["w_in"].shape[0]
    row = lambda a: a.reshape(nl, 1, -1)
    rep = lambda a, n: jnp.repeat(a, n, axis=-1).reshape(nl, 1, -1)
    decay_lanes = lambda a: jnp.pad(a, ((0, 0), (HEADS, LANES - 2 * HEADS))).reshape(nl, 1, LANES)
    p = {}
    w_a, w_gate = _split_in_proj(prm["w_in"])
    p["w_in_a"] = w_a.astype(BF16)
    p["w_gate"] = w_gate.astype(BF16)
    p["mix_norm"] = row(prm["mix_norm"])
    p["gdn_conv_w"] = prm["gdn_conv_w"]
    p["gdn_a_log"] = decay_lanes(prm["gdn_a_log"])
    p["gdn_dt_bias"] = decay_lanes(prm["gdn_dt_bias"])
    p["gdn_norm_w"] = jnp.tile(prm["gdn_norm_w"], (1, HEADS)).reshape(nl, 1, BW)
    lr = row(prm["s5_lambda_re"])
    li = row(prm["s5_lambda_im"])
    ldt = rep(prm["s5_log_dt"], S5_STATE)
    bre = _block_diag(jnp.swapaxes(prm["s5_b_re"], 2, 3))
    bim = _block_diag(jnp.swapaxes(prm["s5_b_im"], 2, 3))
    cre = _block_diag(jnp.swapaxes(prm["s5_c_re"], 2, 3))
    cim = _block_diag(jnp.swapaxes(prm["s5_c_im"], 2, 3))
    p["s5_bbar"], p["s5_cmat"], p["s5_ap"] = _s5_prep(lr, li, ldt, bre, bim, cre, cim, int(math.log2(ts)))
    p["s5_d"] = row(prm["s5_d"])
    p["s5_w_glu"] = prm["s5_w_glu"].astype(BF16)
    p["s5_b_glu"] = row(prm["s5_b_glu"])
    p["lru_conv_w"] = prm["lru_conv_w"]
    p["lru_conv_b"] = row(prm["lru_conv_b"])
    p["lru_w_a"] = _block_diag(prm["lru_w_a"]).astype(BF16)
    p["lru_w_i"] = _block_diag(prm["lru_w_i"]).astype(BF16)
    p["lru_b_a"] = row(prm["lru_b_a"])
    p["lru_b_i"] = row(prm["lru_b_i"])
    p["lru_lambda"] = row(prm["lru_lambda"])
    p["ret_norm_w"] = row(prm["ret_norm_w"])
    p["w_branch"] = prm["w_branch"].astype(BF16)
    p["w_out"] = prm["w_out"].astype(BF16)
    for f in ("ffn1", "ffn2"):
        p[f + "_norm"] = row(prm[f + "_norm"])
        for w in ("w_gate", "w_up", "w_down"):
            p[f + "_" + w] = prm[f + "_" + w].astype(BF16)
    return p


def _trunk(x, positions, prm, final_norm, ts):
    b, s, d = x.shape
    nl = prm["w_in"].shape[0]
    p = _prepare(prm, ts)
    cos, sin = _rope_tables(positions.astype(F32).reshape(b * s, 1))
    cos3, sin3 = cos.reshape(b, s, LANES), sin.reshape(b, s, LANES)
    fw = final_norm.reshape(1, d)
    x2 = x.reshape(b * s, d)
    for l in range(nl):
        x2 = _ffn(x2, l, p["ffn1_norm"], p["ffn1_w_gate"], p["ffn1_w_up"], p["ffn1_w_down"], fw, False)
        br = _branches(x2.reshape(b, s, d), cos3, sin3, l, p, ts)
        x2 = _merge(x2, br.reshape(b * s, N_BRANCH * BW), l, p["mix_norm"], p["w_gate"],
                    p["w_branch"], p["w_out"])
        x2 = _ffn(x2, l, p["ffn2_norm"], p["ffn2_w_gate"], p["ffn2_w_up"], p["ffn2_w_down"], fw,
                  l == nl - 1)
    return x2.reshape(b, s, d)


def kernel(x, positions, ffn1_norm, ffn1_w_gate, ffn1_w_up, ffn1_w_down, mix_norm, w_in, gdn_conv_w, gdn_a_log, gdn_dt_bias, gdn_norm_w, s5_lambda_re, s5_lambda_im, s5_b_re, s5_b_im, s5_c_re, s5_c_im, s5_d, s5_log_dt, s5_w_glu, s5_b_glu, lru_conv_w, lru_conv_b, lru_w_a, lru_b_a, lru_w_i, lru_b_i, lru_lambda, ret_norm_w, w_branch, w_out, ffn2_norm, ffn2_w_gate, ffn2_w_up, ffn2_w_down, final_norm):
    prm = dict(
        ffn1_norm=ffn1_norm, ffn1_w_gate=ffn1_w_gate, ffn1_w_up=ffn1_w_up, ffn1_w_down=ffn1_w_down,
        mix_norm=mix_norm, w_in=w_in, gdn_conv_w=gdn_conv_w, gdn_a_log=gdn_a_log,
        gdn_dt_bias=gdn_dt_bias, gdn_norm_w=gdn_norm_w, s5_lambda_re=s5_lambda_re,
        s5_lambda_im=s5_lambda_im, s5_b_re=s5_b_re, s5_b_im=s5_b_im, s5_c_re=s5_c_re, s5_c_im=s5_c_im,
        s5_d=s5_d, s5_log_dt=s5_log_dt, s5_w_glu=s5_w_glu, s5_b_glu=s5_b_glu, lru_conv_w=lru_conv_w,
        lru_conv_b=lru_conv_b, lru_w_a=lru_w_a, lru_b_a=lru_b_a, lru_w_i=lru_w_i, lru_b_i=lru_b_i,
        lru_lambda=lru_lambda, ret_norm_w=ret_norm_w, w_branch=w_branch, w_out=w_out,
        ffn2_norm=ffn2_norm, ffn2_w_gate=ffn2_w_gate, ffn2_w_up=ffn2_w_up, ffn2_w_down=ffn2_w_down)
    return _trunk(x, positions, prm, final_norm, SEQ_TILE)
```

```python
import functools
import math

import numpy as np
import jax
import jax.numpy as jnp
from jax import lax
from jax.experimental import pallas as pl
from jax.experimental.pallas import tpu as pltpu

F32 = jnp.float32
BF16 = jnp.bfloat16

D_MODEL = 1024
D_FF = 2816
NORM_EPS = 1e-6
N_BRANCH = 4
HEADS = 4
HEAD_DIM = 64
GDN_CONV = 4
GDN_CHUNK = 64
S5_GROUPS = 16
S5_GROUP_WIDTH = 16
S5_STATE = 64
S5_WIDTH = S5_GROUPS * S5_GROUP_WIDTH
S5_NSTATE = S5_GROUPS * S5_STATE
LRU_WIDTH = 256
LRU_BLOCKS = 4
LRU_CONV = 4
LRU_C = 8.0
ROPE_BASE = 10000.0
BW = 256

LANES = 128
SUB = 8
HALO = SUB

SEQ_TILE = 256
TOK_TILE = 1024
FF_CHUNK = 256
VMEM_LIMIT = 56 * 1024 * 1024

C_QKV = 0
C_Z = C_QKV + 3 * BW
C_BA = C_Z + BW
C_S5 = C_BA + LANES
C_LX = C_S5 + BW
C_LY = C_LX + BW
C_RQ = C_LY + BW
C_RK = C_RQ + BW
C_RV = C_RK + BW
C_RG = C_RV + BW
N_A = C_RG + BW


def _sigmoid(x):
    return 0.5 + 0.5 * jnp.tanh(0.5 * x)


def _silu(x):
    return x * _sigmoid(x)


def _gelu_tanh(x):
    return 0.5 * x * (1.0 + jnp.tanh(math.sqrt(2.0 / math.pi) * (x + 0.044715 * (x * x * x))))


def _softplus(x):
    return jnp.maximum(x, 0.0) + jnp.log(1.0 + jnp.exp(-jnp.abs(x)))


def _rms(x, w):
    return x * lax.rsqrt(jnp.mean(x * x, axis=-1, keepdims=True) + NORM_EPS) * w


def _dot(a, b):
    return jnp.dot(a.astype(BF16), b.astype(BF16), preferred_element_type=F32)


def _dot_nt(a, b):
    return lax.dot_general(a.astype(BF16), b.astype(BF16), (((1,), (1,)), ((), ())),
                           preferred_element_type=F32)


def _dot_tn(a, b):
    return lax.dot_general(a.astype(BF16), b.astype(BF16), (((0,), (0,)), ((), ())),
                           preferred_element_type=F32)


def _split_bf16(x, n):
    terms = []
    for _ in range(n - 1):
        t = x.astype(BF16)
        terms.append(t)
        x = x - t.astype(F32)
    terms.append(x.astype(BF16))
    return terms


def _dot_wide_rhs(m_bf16, x, n):
    return sum(jnp.dot(m_bf16, t, preferred_element_type=F32) for t in _split_bf16(x, n))


def _dot_wide_lhs(x, m_bf16, n):
    return sum(jnp.dot(t, m_bf16, preferred_element_type=F32) for t in _split_bf16(x, n))


def _iota2(shape, axis):
    return lax.broadcasted_iota(jnp.int32, shape, axis)


def _head_of(i):
    return lax.shift_right_logical(i, 6)


def _qhead_of(i):
    return lax.shift_right_logical(lax.bitwise_and(i, LANES - 1), 5)


def _ffn_kernel(x_ref, nw_ref, wg_ref, wu_ref, wd_ref, fw_ref, o_ref, *, final):
    x = x_ref[...]
    hn = _rms(x, nw_ref[...]).astype(BF16)
    acc = jnp.zeros(x.shape, F32)
    for c in range(D_FF // FF_CHUNK):
        cs = slice(c * FF_CHUNK, (c + 1) * FF_CHUNK)
        g = jnp.dot(hn, wg_ref[:, cs], preferred_element_type=F32)
        u = jnp.dot(hn, wu_ref[:, cs], preferred_element_type=F32)
        a = (_silu(g) * u).astype(BF16)
        acc = acc + jnp.dot(a, wd_ref[cs, :], preferred_element_type=F32)
    y = x + 0.5 * acc
    if final:
        y = _rms(y, fw_ref[...])
    o_ref[...] = y


def _ffn(x2, layer, nw, wg, wu, wd, fw, final):
    t = x2.shape[0]
    tm = min(TOK_TILE, t)
    wspec = lambda shape: pl.BlockSpec((None,) + shape, lambda i: (layer, 0, 0),
                                       pipeline_mode=pl.Buffered(1))
    return pl.pallas_call(
        functools.partial(_ffn_kernel, final=final),
        grid=(t // tm,),
        in_specs=[
            pl.BlockSpec((tm, D_MODEL), lambda i: (i, 0)),
            pl.BlockSpec((None, 1, D_MODEL), lambda i: (layer, 0, 0)),
            wspec((D_MODEL, D_FF)),
            wspec((D_MODEL, D_FF)),
            wspec((D_FF, D_MODEL)),
            pl.BlockSpec((1, D_MODEL), lambda i: (0, 0)),
        ],
        out_specs=pl.BlockSpec((tm, D_MODEL), lambda i: (i, 0)),
        out_shape=jax.ShapeDtypeStruct((t, D_MODEL), F32),
        compiler_params=pltpu.CompilerParams(
            dimension_semantics=("arbitrary",), vmem_limit_bytes=VMEM_LIMIT),
        name="ffn",
    )(x2, nw, wg, wu, wd, fw)


def _rope_kernel(pos_ref, freq_ref, cos_ref, sin_ref):
    ang = pos_ref[...] * freq_ref[...]
    cos_ref[...] = jnp.cos(ang)
    sin_ref[...] = jnp.sin(ang)


def _rope_tables(pos_f32):
    t = pos_f32.shape[0]
    rows = min(1024, t)
    half = HEAD_DIM // 2
    freq = np.power(ROPE_BASE, -np.arange(half, dtype=np.float32) / half).astype(np.float32)
    freq = jnp.asarray(np.tile(freq, LANES // half)[None, :])
    return pl.pallas_call(
        _rope_kernel,
        grid=(t // rows,),
        in_specs=[pl.BlockSpec((rows, 1), lambda i: (i, 0)),
                  pl.BlockSpec((1, LANES), lambda i: (0, 0))],
        out_specs=[pl.BlockSpec((rows, LANES), lambda i: (i, 0)),
                   pl.BlockSpec((rows, LANES), lambda i: (i, 0))],
        out_shape=[jax.ShapeDtypeStruct((t, LANES), F32)] * 2,
        name="rope_tables",
    )(pos_f32, freq)


def _s5_re(j):
    return slice(2 * j * LANES, (2 * j + 1) * LANES)


def _s5_im(j):
    return slice((2 * j + 1) * LANES, (2 * j + 2) * LANES)


def _pow_rows(nsteps):
    return SUB * ((nsteps + SUB - 1) // SUB)


def _s5_prep_kernel(lr_ref, li_ref, ldt_ref, bre_ref, bim_ref, cre_ref, cim_ref,
                    bbar_ref, cmat_ref, tab_ref, *, nsteps):
    lr, li = lr_ref[...], li_ref[...]
    dt = jnp.exp(ldt_ref[...])
    mag = jnp.exp(lr * dt)
    ar, ai = mag * jnp.cos(li * dt), mag * jnp.sin(li * dt)
    den = lr * lr + li * li
    nr = ar - 1.0
    cr = (nr * lr + ai * li) / den
    ci = (ai * lr - nr * li) / den
    bre, bim = bre_ref[...], bim_ref[...]
    bbar_re = (cr * bre - ci * bim).astype(BF16)
    bbar_im = (cr * bim + ci * bre).astype(BF16)
    cre = cre_ref[...].astype(BF16)
    cim = (-cim_ref[...]).astype(BF16)
    for j in range(S5_NSTATE // LANES):
        src = slice(j * LANES, (j + 1) * LANES)
        bbar_ref[:, _s5_re(j)] = bbar_re[:, src]
        bbar_ref[:, _s5_im(j)] = bbar_im[:, src]
        cmat_ref[_s5_re(j), :] = cre[src, :]
        cmat_ref[_s5_im(j), :] = cim[src, :]

    def put(r, vr, vi):
        for j in range(S5_NSTATE // LANES):
            src = slice(j * LANES, (j + 1) * LANES)
            tab_ref[r:r + 1, _s5_re(j)] = vr[:, src]
            tab_ref[r:r + 1, _s5_im(j)] = vi[:, src]

    zero = jnp.zeros_like(ar)
    npow = _pow_rows(nsteps)
    pows = []
    pr, pi = ar, ai
    for k in range(npow):
        put(k, pr, pi)
        pows.append((pr, pi))
        pr, pi = pr * pr - pi * pi, 2.0 * pr * pi
    pr, pi = ar, ai
    for r in range(SUB):
        put(npow + r, pr, pi)
        pr, pi = pr * ar - pi * ai, pr * ai + pi * ar
    for k in range(3):
        for r in range(SUB):
            keep = r >= (1 << k)
            put(npow + SUB + k * SUB + r, pows[k][0] if keep else zero, pows[k][1] if keep else zero)


def _s5_prep(lr, li, ldt, bre, bim, cre, cim, nsteps):
    nl = lr.shape[0]
    nrows = _pow_rows(nsteps) + 4 * SUB
    vec = pl.BlockSpec((None, 1, S5_NSTATE), lambda l: (l, 0, 0))
    return pl.pallas_call(
        functools.partial(_s5_prep_kernel, nsteps=nsteps),
        grid=(nl,),
        in_specs=[vec, vec, vec,
                  pl.BlockSpec((None, S5_WIDTH, S5_NSTATE), lambda l: (l, 0, 0)),
                  pl.BlockSpec((None, S5_WIDTH, S5_NSTATE), lambda l: (l, 0, 0)),
                  pl.BlockSpec((None, S5_NSTATE, S5_WIDTH), lambda l: (l, 0, 0)),
                  pl.BlockSpec((None, S5_NSTATE, S5_WIDTH), lambda l: (l, 0, 0))],
        out_specs=[pl.BlockSpec((None, S5_WIDTH, 2 * S5_NSTATE), lambda l: (l, 0, 0)),
                   pl.BlockSpec((None, 2 * S5_NSTATE, S5_WIDTH), lambda l: (l, 0, 0)),
                   pl.BlockSpec((None, nrows, 2 * S5_NSTATE), lambda l: (l, 0, 0))],
        out_shape=[jax.ShapeDtypeStruct((nl, S5_WIDTH, 2 * S5_NSTATE), BF16),
                   jax.ShapeDtypeStruct((nl, 2 * S5_NSTATE, S5_WIDTH), BF16),
                   jax.ShapeDtypeStruct((nl, nrows, 2 * S5_NSTATE), F32)],
        name="s5_prep",
    )(lr, li, ldt, bre, bim, cre, cim)


def _mixer_kernel(
        x_ref, cos_ref, sin_ref, nw_ref, win_ref,
        gconv_ref, galog_ref, gdtb_ref, gnorm_ref,
        bbar_ref, cmat_ref, tab_ref, s5d_ref, wglu_ref, bglu_ref,
        lcw_ref, lcb_ref, lwa_ref, lba_ref, lwi_ref, lbi_ref, llam_ref,
        rnorm_ref, rintra_ref, rxi_ref, rzeta_ref, rdec_ref,
        wgate_ref, wbr_ref, wout_ref,
        o_ref,
        gbuf, lbuf, qkv_s, gcb_s, og_s, xs_s, ge_s, lg_s, gate_s, mrg_s, gstate, s5c, lruc, rstate,
        *, ts, nsteps):
    first = pl.program_id(1) == 0

    @pl.when(first)
    def _():
        gbuf[...] = jnp.zeros(gbuf.shape, F32)
        lbuf[...] = jnp.zeros(lbuf.shape, F32)
        gstate[...] = jnp.zeros(gstate.shape, F32)
        rstate[...] = jnp.zeros(rstate.shape, F32)
        s5c[...] = jnp.zeros(s5c.shape, F32)
        lruc[...] = jnp.zeros(lruc.shape, F32)

    hn = _rms(x_ref[...], nw_ref[...]).astype(BF16)

    def proj(c0, width):
        return jnp.dot(hn, win_ref[:, c0:c0 + width], preferred_element_type=F32)

    r2 = _iota2((BW, BW), 0)
    c2 = _iota2((BW, BW), 1)
    same_head = _head_of(r2) == _head_of(c2)
    ones_bd = jnp.where(same_head, 1.0, 0.0).astype(BF16)
    lane = _iota2((1, BW), 1)

    def head_sum(v, terms=1):
        return _dot_wide_lhs(v, ones_bd, terms)

    def causal_conv(cur, hist_ref, w_ref, taps):
        ext = jnp.concatenate([hist_ref[...], cur], axis=0)
        hist_ref[...] = cur[ts - HALO:ts, :]
        y = w_ref[taps - 1:taps, :] * cur
        for d in range(1, taps):
            y = y + w_ref[taps - 1 - d:taps - d, :] * pltpu.roll(ext, d, 0)[HALO:, :]
        return y

    def merge_gate_piece(n, c):
        c0 = n * D_MODEL + c * BW
        gate_s[n, :, c * BW:(c + 1) * BW] = _sigmoid(
            jnp.dot(hn, wgate_ref[:, c0:c0 + BW], preferred_element_type=F32))

    def merge_branch(n, out_n):
        term = gate_s[n] * jnp.dot(out_n.astype(BF16), wbr_ref[n], preferred_element_type=F32)
        if merged["started"]:
            mrg_s[...] = mrg_s[...] + term
        else:
            mrg_s[...] = term
            merged["started"] = True

    merged = {"started": False}

    def gdn_inputs():
        qkv = _silu(causal_conv(proj(C_QKV, 3 * BW), gbuf, gconv_ref, GDN_CONV))
        q, k_, v = qkv[:, 0:BW], qkv[:, BW:2 * BW], qkv[:, 2 * BW:3 * BW]
        qkv_s[:, 0:BW] = q * lax.rsqrt(head_sum(q * q) + 1e-6) * (HEAD_DIM ** -0.5)
        qkv_s[:, BW:2 * BW] = k_ * lax.rsqrt(head_sum(k_ * k_) + 1e-6)
        qkv_s[:, 2 * BW:3 * BW] = v
        ba = proj(C_BA, LANES)
        beta = _sigmoid(ba)
        g = -jnp.exp(galog_ref[...]) * _softplus(ba + gdtb_ref[...])
        rt = _iota2((ts, ts), 0)
        ct = _iota2((ts, ts), 1)
        mchunk = jnp.where((_head_of(rt) == _head_of(ct)) & (ct <= rt), 1.0, 0.0).astype(BF16)
        gc = _dot_wide_rhs(mchunk, g, 3)
        for h in range(HEADS):
            gcb_s[:, h * LANES:(h + 1) * LANES] = jnp.broadcast_to(gc[:, HEADS + h:HEADS + h + 1], (ts, LANES))
            gcb_s[:, (HEADS + h) * LANES:(HEADS + h + 1) * LANES] = jnp.broadcast_to(beta[:, h:h + 1], (ts, LANES))

    hmask = [jnp.where(_head_of(lane) == h, 1.0, 0.0) for h in range(HEADS)]
    bd_b = jnp.where(same_head, 1.0, 0.0).astype(BF16)
    rc = _iota2((GDN_CHUNK, BW), 0)
    jc = lax.bitwise_and(_iota2((GDN_CHUNK, BW), 1), HEAD_DIM - 1)
    incl = jc <= rc
    strict = jc < rc
    eye = jnp.where(jc == rc, 1.0, 0.0)
    low_lane = _iota2((GDN_CHUNK, LANES), 1) < HEAD_DIM

    def stack(m):
        mb = m.astype(BF16)
        return jnp.concatenate([mb] * HEADS, axis=0) * bd_b

    def per_head_lanes(blocks):
        return jnp.concatenate([jnp.where(low_lane, blocks[0], blocks[1]),
                                jnp.where(low_lane, blocks[2], blocks[3])], axis=1)

    nchunk = ts // GDN_CHUNK
    crow = [slice(n * GDN_CHUNK, (n + 1) * GDN_CHUNK) for n in range(nchunk)]
    nt_dims = (((1,), (1,)), ((), ()))
    chunks = [dict() for _ in range(nchunk)]

    def gdn_local(n):
        qc = qkv_s[crow[n], 0:BW]
        kc = qkv_s[crow[n], BW:2 * BW]
        vc = qkv_s[crow[n], 2 * BW:3 * BW]
        gh = [gcb_s[crow[n], h * LANES:(h + 1) * LANES] for h in range(HEADS)]
        bh = [gcb_s[crow[n], (HEADS + h) * LANES:(HEADS + h + 1) * LANES] for h in range(HEADS)]
        g64 = per_head_lanes(gh)
        b64 = per_head_lanes(bh)
        grow = jnp.concatenate(gh, axis=0).T[0:GDN_CHUNK, :]
        decay = jnp.where(incl, jnp.exp(jnp.where(incl, g64 - grow, 0.0)), 0.0)
        ks_b = stack(kc)
        kk = lax.dot_general(kc.astype(BF16), ks_b, nt_dims, preferred_element_type=F32)
        qk = lax.dot_general(qc.astype(BF16), ks_b, nt_dims, preferred_element_type=F32)
        low = jnp.where(strict, b64 * kk * decay, 0.0)
        e64 = jnp.exp(g64)
        glast = g64[GDN_CHUNK - 1:GDN_CHUNK, :]
        chunks[n].update(
            tinv=eye - low, p=low, p_bd=stack(low),
            rhs_v=stack(vc * b64), rhs_k=stack(kc * (b64 * e64)), attn_b=(qk * decay).astype(BF16),
            q_dec=(qc * e64).astype(BF16), k_dec=(kc * jnp.exp(glast - g64)).astype(BF16),
            cdec=jnp.exp(glast))

    def gdn_inverse_step(n):
        c = chunks[n]
        c["p"] = jnp.dot(c["p"].astype(BF16), c["p_bd"], preferred_element_type=F32)
        c["p_bd"] = stack(c["p"])
        c["tinv"] = c["tinv"] + jnp.dot(c["tinv"].astype(BF16), c["p_bd"], preferred_element_type=F32)

    def gdn_solve(n):
        c = chunks[n]
        tinv_b = c["tinv"].astype(BF16)
        c["u"] = jnp.dot(tinv_b, c["rhs_v"], preferred_element_type=F32)
        c["w"] = jnp.dot(tinv_b, c["rhs_k"], preferred_element_type=F32).astype(BF16)

    def gdn_state_step(n):
        c = chunks[n]
        st = gstate[...]
        st_b = st.astype(BF16)
        v_new = c["u"] - jnp.dot(c["w"], st_b, preferred_element_type=F32)
        kv = lax.dot_general(c["k_dec"], v_new.astype(BF16), (((0,), (0,)), ((), ())),
                             preferred_element_type=F32)
        gstate[...] = st * c["cdec"] + jnp.where(same_head, kv, 0.0)
        og_s[crow[n], :] = (jnp.dot(c["q_dec"], st_b, preferred_element_type=F32)
                            + jnp.dot(c["attn_b"], stack(v_new), preferred_element_type=F32))

    def gdn_output():
        og = og_s[...]
        og = og * lax.rsqrt(head_sum(og * og) * (1.0 / HEAD_DIM) + NORM_EPS) * gnorm_ref[...]
        merge_branch(0, og * _silu(proj(C_Z, BW)))

    ngrp = ts // SUB
    npow = _pow_rows(nsteps)
    rowe = _iota2((ngrp, LANES), 0)
    s5 = {}

    def s5_inputs():
        s5["u"] = proj(C_S5, BW)
        s5["u_b"] = s5["u"].astype(BF16)

    def s5_slab(j, between=()):
        between = list(between)
        lr_, li_ = _s5_re(j), _s5_im(j)
        bu = jnp.dot(s5["u_b"], bbar_ref[:, 2 * j * LANES:(2 * j + 2) * LANES], preferred_element_type=F32)
        xr = bu[:, 0:LANES].reshape(ngrp, SUB, LANES)
        xi = bu[:, LANES:].reshape(ngrp, SUB, LANES)
        for s in range(3):
            m0 = npow + SUB + s * SUB
            mr, mi = tab_ref[m0:m0 + SUB, lr_][None], tab_ref[m0:m0 + SUB, li_][None]
            sr, si = pltpu.roll(xr, 1 << s, 1), pltpu.roll(xi, 1 << s, 1)
            xr, xi = xr + (mr * sr - mi * si), xi + (mr * si + mi * sr)
        if between:
            between.pop(0)()
        ge_s[2 * j] = xr.reshape(ts, LANES)
        ge_s[2 * j + 1] = xi.reshape(ts, LANES)
        er = ge_s[2 * j, pl.ds(SUB - 1, ngrp, stride=SUB), :]
        ei = ge_s[2 * j + 1, pl.ds(SUB - 1, ngrp, stride=SUB), :]
        cr, ci = s5c[0:1, lr_], s5c[0:1, li_]
        a8r, a8i = tab_ref[3:4, lr_], tab_ref[3:4, li_]
        er = er + jnp.where(rowe == 0, a8r * cr - a8i * ci, 0.0)
        ei = ei + jnp.where(rowe == 0, a8r * ci + a8i * cr, 0.0)
        for s in range(3, nsteps):
            d = 1 << (s - 3)
            pr, pi = tab_ref[s:s + 1, lr_], tab_ref[s:s + 1, li_]
            sr = jnp.where(rowe >= d, pltpu.roll(er, d, 0), 0.0)
            si = jnp.where(rowe >= d, pltpu.roll(ei, d, 0), 0.0)
            er, ei = er + (pr * sr - pi * si), ei + (pr * si + pi * sr)
        s5c[0:1, lr_] = er[ngrp - 1:ngrp, :]
        s5c[0:1, li_] = ei[ngrp - 1:ngrp, :]
        pr_ = jnp.where(rowe == 0, cr, pltpu.roll(er, 1, 0))
        pi_ = jnp.where(rowe == 0, ci, pltpu.roll(ei, 1, 0))
        tr, ti = tab_ref[npow:npow + SUB, lr_], tab_ref[npow:npow + SUB, li_]
        if between:
            between.pop(0)()
        for gi in range(ngrp):
            rs = slice(gi * SUB, (gi + 1) * SUB)
            cgr, cgi = pr_[gi:gi + 1, :], pi_[gi:gi + 1, :]
            xs_s[rs, lr_] = xr[gi] + (tr * cgr - ti * cgi)
            xs_s[rs, li_] = xi[gi] + (tr * cgi + ti * cgr)

    def s5_output():
        y5 = _dot(xs_s[...], cmat_ref[...]) + s5d_ref[...] * s5["u"]
        y5 = _gelu_tanh(y5)
        y5 = y5 * _sigmoid(_dot(y5, wglu_ref[...]) + bglu_ref[...])
        merge_branch(1, y5)

    rsub = lax.broadcasted_iota(jnp.int32, (ngrp, SUB, BW), 1)
    rowg = _iota2((ngrp, BW), 0)
    lru = {}

    def lru_inputs():
        xc = causal_conv(proj(C_LX, BW), lbuf, lcw_ref, LRU_CONV) + lcb_ref[...]
        rg = _sigmoid(_dot(xc, lwa_ref[...]) + lba_ref[...])
        ig = _sigmoid(_dot(xc, lwi_ref[...]) + lbi_ref[...])
        log_a = -LRU_C * rg * _softplus(-llam_ref[...])
        mult = jnp.sqrt(1.0 - jnp.exp(2.0 * log_a))
        lru["a"] = jnp.exp(log_a).reshape(ngrp, SUB, BW)
        lru["x"] = (xc * ig * mult).reshape(ngrp, SUB, BW)

    def lru_group_step(s):
        d = 1 << s
        sa = jnp.where(rsub >= d, pltpu.roll(lru["a"], d, 1), 1.0)
        sx = jnp.where(rsub >= d, pltpu.roll(lru["x"], d, 1), 0.0)
        lru["x"] = lru["x"] + lru["a"] * sx
        lru["a"] = lru["a"] * sa

    def lru_finish():
        a2, x2 = lru["a"].reshape(ts, BW), lru["x"].reshape(ts, BW)
        halves = BW // LANES
        for c in range(halves):
            lg_s[c] = a2[:, c * LANES:(c + 1) * LANES]
            lg_s[halves + c] = x2[:, c * LANES:(c + 1) * LANES]
        ends = pl.ds(SUB - 1, ngrp, stride=SUB)
        ea = jnp.concatenate([lg_s[c, ends, :] for c in range(halves)], axis=1)
        ex = jnp.concatenate([lg_s[halves + c, ends, :] for c in range(halves)], axis=1)
        carry = lruc[0:1, :]
        ex = ex + jnp.where(rowg == 0, ea * carry, 0.0)
        for s in range(nsteps - 3):
            d = 1 << s
            sa = jnp.where(rowg >= d, pltpu.roll(ea, d, 0), 1.0)
            sx = jnp.where(rowg >= d, pltpu.roll(ex, d, 0), 0.0)
            ex = ex + ea * sx
            ea = ea * sa
        lruc[0:1, :] = ex[ngrp - 1:ngrp, :]
        enter = jnp.where(rowg == 0, carry, pltpu.roll(ex, 1, 0))
        hx = jnp.concatenate(
            [lru["x"][gi] + lru["a"][gi] * enter[gi:gi + 1, :] for gi in range(ngrp)], axis=0)
        merge_branch(2, hx * _gelu_tanh(proj(C_LY, BW)))

    ret = {}
    qmask = [jnp.where(_qhead_of(lane) == h, 1.0, 0.0) for h in range(HEADS)]

    def ret_inputs():
        cos, sin = cos_ref[...], sin_ref[...]

        def rope(t):
            t1, t2 = t[:, 0:LANES], t[:, LANES:]
            return jnp.concatenate([t1 * cos - t2 * sin, t1 * sin + t2 * cos], axis=1)

        rq = rope(proj(C_RQ, BW))
        rk = rope(proj(C_RK, BW)) * (HEAD_DIM ** -0.5)
        rv_b = proj(C_RV, BW).astype(BF16)
        ret.update(rq=rq, rk_b=rk.astype(BF16), rv_b=rv_b,
                   o=_dot(rq * rxi_ref[...], rstate[...]))
        qv_same = _qhead_of(r2) == _head_of(c2)
        rstate[...] = rstate[...] * rdec_ref[...] + jnp.where(
            qv_same, _dot_tn(rk * rzeta_ref[...], rv_b), 0.0)

    def ret_head(h):
        sc = _dot_nt(ret["rq"] * qmask[h], ret["rk_b"]) * rintra_ref[h]
        ret["o"] = ret["o"] + hmask[h] * _dot(sc, ret["rv_b"])

    def ret_output():
        orr = ret["o"]
        mu = head_sum(orr, 2) * (1.0 / HEAD_DIM)
        cen = orr - mu
        var = head_sum(cen * cen) * (1.0 / HEAD_DIM)
        on = cen * lax.rsqrt(var + NORM_EPS) * rnorm_ref[...]
        merge_branch(3, _silu(proj(C_RG, BW)) * on)

    P = functools.partial
    nslab = S5_NSTATE // LANES
    pieces = [P(merge_gate_piece, n, c) for n in range(N_BRANCH) for c in range(D_MODEL // BW)]
    per_slab = len(pieces) // nslab
    slab = [P(s5_slab, j, pieces[j * per_slab:(j + 1) * per_slab]) for j in range(nslab)]
    fill = (slab[0:4] + [lru_inputs] + slab[4:8] + [P(lru_group_step, s) for s in range(3)] + [ret_inputs]
            + [P(ret_head, h) for h in range(HEADS)] + [s5_output, lru_finish, ret_output])
    links = ([[P(gdn_inverse_step, n) for n in range(nchunk)] for _ in range(5)]
             + [[P(gdn_solve, n) for n in range(nchunk)]]
             + [[P(gdn_state_step, n)] for n in range(nchunk)])
    per_link = -(-len(fill) // len(links))
    s5_inputs()
    gdn_inputs()
    for n in range(nchunk):
        gdn_local(n)
    for link in links:
        for work in link:
            work()
        for work in fill[:per_link]:
            work()
        fill = fill[per_link:]
    for work in fill:
        work()
    gdn_output()
    o_ref[...] = x_ref[...] + jnp.dot(mrg_s[...].astype(BF16), wout_ref[...], preferred_element_type=F32)


def _retention_tables(ts):
    hh = np.arange(HEADS, dtype=np.float64)
    log_gamma = np.log1p(-np.exp2(-5.0 - hh))
    idx = np.arange(ts, dtype=np.float64)
    rel = idx[:, None] - idx[None, :]
    intra = np.where(rel >= 0, np.exp(np.where(rel >= 0, rel, 0.0)[None] * log_gamma[:, None, None]), 0.0)
    xi = np.exp((idx + 1.0)[None] * log_gamma[:, None])
    zeta = np.exp((ts - 1.0 - idx)[None] * log_gamma[:, None])
    cdec = np.exp(ts * log_gamma)
    qhead = (np.arange(BW) % LANES) // (HEAD_DIM // 2)
    vhead = np.arange(BW) // HEAD_DIM
    xi_q = xi[qhead].T
    zeta_q = zeta[qhead].T
    dec = np.where(qhead[:, None] == vhead[None, :], cdec[qhead][:, None], 0.0)
    f = lambda a: jnp.asarray(a.astype(np.float32))
    return f(intra), f(xi_q), f(zeta_q), f(dec)


def _mixer(x3, cos3, sin3, layer, p, ts):
    b, s, _ = x3.shape
    nsteps = int(math.log2(ts))
    assert 1 << nsteps == ts and s % ts == 0 and ts % GDN_CHUNK == 0
    rintra, rxi, rzeta, rdec = _retention_tables(ts)

    def lspec(shape, single=False):
        nd = len(shape)
        kw = {"pipeline_mode": pl.Buffered(1)} if single else {}
        return pl.BlockSpec((None,) + shape, lambda bi, si: (layer,) + (0,) * nd, **kw)

    def cspec(shape):
        nd = len(shape)
        return pl.BlockSpec(shape, lambda bi, si: (0,) * nd)

    tile = lambda w: pl.BlockSpec((None, ts, w), lambda bi, si: (bi, si, 0))
    in_specs = [
        tile(D_MODEL), tile(LANES), tile(LANES),
        lspec((1, D_MODEL)), lspec((D_MODEL, N_A), single=True),
        lspec((GDN_CONV, 3 * BW)), lspec((1, LANES)), lspec((1, LANES)), lspec((1, BW)),
        lspec((S5_WIDTH, 2 * S5_NSTATE)), lspec((2 * S5_NSTATE, S5_WIDTH)),
        lspec((p["s5_tab"].shape[1], 2 * S5_NSTATE)), lspec((1, BW)), lspec((BW, BW)), lspec((1, BW)),
        lspec((LRU_CONV, BW)), lspec((1, BW)), lspec((BW, BW)), lspec((1, BW)), lspec((BW, BW)),
        lspec((1, BW)), lspec((1, BW)),
        lspec((1, BW)), cspec((HEADS, ts, ts)), cspec((ts, BW)), cspec((ts, BW)), cspec((BW, BW)),
        lspec((D_MODEL, N_BRANCH * D_MODEL), single=True), lspec((N_BRANCH, BW, D_MODEL), single=True),
        lspec((D_MODEL, D_MODEL), single=True),
    ]
    scratch = [
        pltpu.VMEM((HALO, 3 * BW), F32),
        pltpu.VMEM((HALO, BW), F32),
        pltpu.VMEM((ts, 3 * BW), F32),
        pltpu.VMEM((ts, 2 * HEADS * LANES), F32),
        pltpu.VMEM((ts, BW), F32),
        pltpu.VMEM((ts, 2 * S5_NSTATE), F32),
        pltpu.VMEM((2 * S5_NSTATE // LANES, ts, LANES), F32),
        pltpu.VMEM((2 * BW // LANES, ts, LANES), F32),
        pltpu.VMEM((N_BRANCH, ts, D_MODEL), F32),
        pltpu.VMEM((ts, D_MODEL), F32),
        pltpu.VMEM((BW, BW), F32),
        pltpu.VMEM((HALO, 2 * S5_NSTATE), F32),
        pltpu.VMEM((HALO, BW), F32),
        pltpu.VMEM((BW, BW), F32),
    ]
    return pl.pallas_call(
        functools.partial(_mixer_kernel, ts=ts, nsteps=nsteps),
        grid=(b, s // ts),
        in_specs=in_specs,
        out_specs=pl.BlockSpec((None, ts, D_MODEL), lambda bi, si: (bi, si, 0)),
        out_shape=jax.ShapeDtypeStruct((b, s, D_MODEL), F32),
        scratch_shapes=scratch,
        compiler_params=pltpu.CompilerParams(
            dimension_semantics=("arbitrary", "arbitrary"), vmem_limit_bytes=VMEM_LIMIT),
        name="mixer",
    )(x3, cos3, sin3, p["mix_norm"], p["w_in_a"],
      p["gdn_conv_w"], p["gdn_a_log"], p["gdn_dt_bias"], p["gdn_norm_w"],
      p["s5_bbar"], p["s5_cmat"], p["s5_tab"], p["s5_d"], p["s5_w_glu"], p["s5_b_glu"],
      p["lru_conv_w"], p["lru_conv_b"], p["lru_w_a"], p["lru_b_a"], p["lru_w_i"], p["lru_b_i"],
      p["lru_lambda"], p["ret_norm_w"], rintra, rxi, rzeta, rdec,
      p["w_gate"], p["w_branch"], p["w_out"])


def _split_in_proj(w_in):
    offs = np.cumsum([0, 3 * BW, BW, HEADS, HEADS, BW, BW, BW, BW, BW, BW, BW])
    qkv, _, beta, _, s5, _, _, rq, rk, rv, _, gate0 = (int(o) for o in offs)
    nl, d, _ = w_in.shape

    def rope_split(w):
        return w.reshape(nl, d, HEADS, 2, HEAD_DIM // 2).transpose(0, 1, 3, 2, 4).reshape(nl, d, BW)

    ba = jnp.pad(w_in[:, :, beta:s5], ((0, 0), (0, 0), (0, LANES - 2 * HEADS)))
    w_a = jnp.concatenate([w_in[:, :, qkv:beta], ba, w_in[:, :, s5:rq], rope_split(w_in[:, :, rq:rk]),
                           rope_split(w_in[:, :, rk:rv]), w_in[:, :, rv:gate0]], axis=2)
    assert w_a.shape[2] == N_A
    return w_a, w_in[:, :, gate0:]


def _block_diag(blocks):
    nl, n, r, c = blocks.shape
    eye = jnp.eye(n, dtype=blocks.dtype)
    return (blocks[:, :, :, None, :] * eye[None, :, None, :, None]).reshape(nl, n * r, n * c)


def _prepare(prm, ts):
    nl = prm["w_in"].shape[0]
    row = lambda a: a.reshape(nl, 1, -1)
    rep = lambda a, n: jnp.repeat(a, n, axis=-1).reshape(nl, 1, -1)
    decay_lanes = lambda a: jnp.pad(a, ((0, 0), (HEADS, LANES - 2 * HEADS))).reshape(nl, 1, LANES)
    p = {}
    w_a, w_gate = _split_in_proj(prm["w_in"])
    p["w_in_a"] = w_a.astype(BF16)
    p["w_gate"] = w_gate.astype(BF16)
    p["mix_norm"] = row(prm["mix_norm"])
    p["gdn_conv_w"] = prm["gdn_conv_w"]
    p["gdn_a_log"] = decay_lanes(prm["gdn_a_log"])
    p["gdn_dt_bias"] = decay_lanes(prm["gdn_dt_bias"])
    p["gdn_norm_w"] = jnp.tile(prm["gdn_norm_w"], (1, HEADS)).reshape(nl, 1, BW)
    lr = row(prm["s5_lambda_re"])
    li = row(prm["s5_lambda_im"])
    ldt = rep(prm["s5_log_dt"], S5_STATE)
    bre = _block_diag(jnp.swapaxes(prm["s5_b_re"], 2, 3))
    bim = _block_diag(jnp.swapaxes(prm["s5_b_im"], 2, 3))
    cre = _block_diag(jnp.swapaxes(prm["s5_c_re"], 2, 3))
    cim = _block_diag(jnp.swapaxes(prm["s5_c_im"], 2, 3))
    p["s5_bbar"], p["s5_cmat"], p["s5_tab"] = _s5_prep(lr, li, ldt, bre, bim, cre, cim, int(math.log2(ts)))
    p["s5_d"] = row(prm["s5_d"])
    p["s5_w_glu"] = prm["s5_w_glu"].astype(BF16)
    p["s5_b_glu"] = row(prm["s5_b_glu"])
    p["lru_conv_w"] = prm["lru_conv_w"]
    p["lru_conv_b"] = row(prm["lru_conv_b"])
    p["lru_w_a"] = _block_diag(prm["lru_w_a"]).astype(BF16)
    p["lru_w_i"] = _block_diag(prm["lru_w_i"]).astype(BF16)
    p["lru_b_a"] = row(prm["lru_b_a"])
    p["lru_b_i"] = row(prm["lru_b_i"])
    p["lru_lambda"] = row(prm["lru_lambda"])
    p["ret_norm_w"] = row(prm["ret_norm_w"])
    p["w_branch"] = prm["w_branch"].astype(BF16)
    p["w_out"] = prm["w_out"].astype(BF16)
    for f in ("ffn1", "ffn2"):
        p[f + "_norm"] = row(prm[f + "_norm"])
        for w in ("w_gate", "w_up", "w_down"):
            p[f + "_" + w] = prm[f + "_" + w].astype(BF16)
    return p


def _trunk(x, positions, prm, final_norm, ts):
    b, s, d = x.shape
    nl = prm["w_in"].shape[0]
    p = _prepare(prm, ts)
    cos, sin = _rope_tables(positions.astype(F32).reshape(b * s, 1))
    cos3, sin3 = cos.reshape(b, s, LANES), sin.reshape(b, s, LANES)
    fw = final_norm.reshape(1, d)
    x2 = x.reshape(b * s, d)
    for l in range(nl):
        x2 = _ffn(x2, l, p["ffn1_norm"], p["ffn1_w_gate"], p["ffn1_w_up"], p["ffn1_w_down"], fw, False)
        x2 = _mixer(x2.reshape(b, s, d), cos3, sin3, l, p, ts).reshape(b * s, d)
        x2 = _ffn(x2, l, p["ffn2_norm"], p["ffn2_w_gate"], p["ffn2_w_up"], p["ffn2_w_down"], fw,
                  l == nl - 1)
    return x2.reshape(b, s, d)


def kernel(x, positions, ffn1_norm, ffn1_w_gate, ffn1_w_up, ffn1_w_down, mix_norm, w_in, gdn_conv_w, gdn_a_log, gdn_dt_bias, gdn_norm_w, s5_lambda_re, s5_lambda_im, s5_b_re, s5_b_im, s5_c_re, s5_c_im, s5_d, s5_log_dt, s5_w_glu, s5_b_glu, lru_conv_w, lru_conv_b, lru_w_a, lru_b_a, lru_w_i, lru_b_i, lru_lambda, ret_norm_w, w_branch, w_out, ffn2_norm, ffn2_w_gate, ffn2_w_up, ffn2_w_down, final_norm):
    prm = dict(
        ffn1_norm=ffn1_norm, ffn1_w_gate=ffn1_w_gate, ffn1_w_up=ffn1_w_up, ffn1_w_down=ffn1_w_down,
        mix_norm=mix_norm, w_in=w_in, gdn_conv_w=gdn_conv_w, gdn_a_log=gdn_a_log,
        gdn_dt_bias=gdn_dt_bias, gdn_norm_w=gdn_norm_w, s5_lambda_re=s5_lambda_re,
        s5_lambda_im=s5_lambda_im, s5_b_re=s5_b_re, s5_b_im=s5_b_im, s5_c_re=s5_c_re, s5_c_im=s5_c_im,
        s5_d=s5_d, s5_log_dt=s5_log_dt, s5_w_glu=s5_w_glu, s5_b_glu=s5_b_glu, lru_conv_w=lru_conv_w,
        lru_conv_b=lru_conv_b, lru_w_a=lru_w_a, lru_b_a=lru_b_a, lru_w_i=lru_w_i, lru_b_i=lru_b_i,
        lru_lambda=lru_lambda, ret_norm_w=ret_norm_w, w_branch=w_branch, w_out=w_out,
        ffn2_norm=ffn2_norm, ffn2_w_gate=ffn2_w_gate, ffn2_w_up=ffn2_w_up, ffn2_w_down=ffn2_w_down)
    return _trunk(x, positions, prm, final_norm, SEQ_TILE)
```

```python
import functools
import math

import numpy as np
import jax
import jax.numpy as jnp
from jax import lax
from jax.experimental import pallas as pl
from jax.experimental.pallas import tpu as pltpu

F32 = jnp.float32
BF16 = jnp.bfloat16

D_MODEL = 1024
D_FF = 2816
NORM_EPS = 1e-6
N_BRANCH = 4
HEADS = 4
HEAD_DIM = 64
GDN_CONV = 4
GDN_CHUNK = 64
S5_GROUPS = 16
S5_GROUP_WIDTH = 16
S5_STATE = 64
S5_WIDTH = S5_GROUPS * S5_GROUP_WIDTH
S5_NSTATE = S5_GROUPS * S5_STATE
LRU_WIDTH = 256
LRU_BLOCKS = 4
LRU_CONV = 4
LRU_C = 8.0
ROPE_BASE = 10000.0
BW = 256

LANES = 128
SUB = 8
HALO = SUB

SEQ_TILE = 256
TOK_TILE = 1024
FF_CHUNK = 256
VMEM_LIMIT = 56 * 1024 * 1024

C_QKV = 0
C_Z = C_QKV + 3 * BW
C_BA = C_Z + BW
C_S5 = C_BA + LANES
C_LX = C_S5 + BW
C_LY = C_LX + BW
C_RQ = C_LY + BW
C_RK = C_RQ + BW
C_RV = C_RK + BW
C_RG = C_RV + BW
N_A = C_RG + BW


def _sigmoid(x):
    return 0.5 + 0.5 * jnp.tanh(0.5 * x)


def _silu(x):
    return x * _sigmoid(x)


def _gelu_tanh(x):
    return 0.5 * x * (1.0 + jnp.tanh(math.sqrt(2.0 / math.pi) * (x + 0.044715 * (x * x * x))))


def _softplus(x):
    return jnp.maximum(x, 0.0) + jnp.log(1.0 + jnp.exp(-jnp.abs(x)))


def _rms(x, w):
    return x * lax.rsqrt(jnp.mean(x * x, axis=-1, keepdims=True) + NORM_EPS) * w


def _dot(a, b):
    return jnp.dot(a.astype(BF16), b.astype(BF16), preferred_element_type=F32)


def _dot_nt(a, b):
    return lax.dot_general(a.astype(BF16), b.astype(BF16), (((1,), (1,)), ((), ())),
                           preferred_element_type=F32)


def _dot_tn(a, b):
    return lax.dot_general(a.astype(BF16), b.astype(BF16), (((0,), (0,)), ((), ())),
                           preferred_element_type=F32)


def _split_bf16(x, n):
    terms = []
    for _ in range(n - 1):
        t = x.astype(BF16)
        terms.append(t)
        x = x - t.astype(F32)
    terms.append(x.astype(BF16))
    return terms


def _dot_wide_rhs(m_bf16, x, n):
    return sum(jnp.dot(m_bf16, t, preferred_element_type=F32) for t in _split_bf16(x, n))


def _dot_wide_lhs(x, m_bf16, n):
    return sum(jnp.dot(t, m_bf16, preferred_element_type=F32) for t in _split_bf16(x, n))


def _iota2(shape, axis):
    return lax.broadcasted_iota(jnp.int32, shape, axis)


def _head_of(i):
    return lax.shift_right_logical(i, 6)


def _qhead_of(i):
    return lax.shift_right_logical(lax.bitwise_and(i, LANES - 1), 5)


def _ffn_kernel(x_ref, nw_ref, wg_ref, wu_ref, wd_ref, fw_ref, o_ref, *, final):
    x = x_ref[...]
    hn = _rms(x, nw_ref[...]).astype(BF16)
    acc = jnp.zeros(x.shape, F32)
    for c in range(D_FF // FF_CHUNK):
        cs = slice(c * FF_CHUNK, (c + 1) * FF_CHUNK)
        g = jnp.dot(hn, wg_ref[:, cs], preferred_element_type=F32)
        u = jnp.dot(hn, wu_ref[:, cs], preferred_element_type=F32)
        a = (_silu(g) * u).astype(BF16)
        acc = acc + jnp.dot(a, wd_ref[cs, :], preferred_element_type=F32)
    y = x + 0.5 * acc
    if final:
        y = _rms(y, fw_ref[...])
    o_ref[...] = y


def _ffn(x2, layer, nw, wg, wu, wd, fw, final):
    t = x2.shape[0]
    tm = min(TOK_TILE, t)
    wspec = lambda shape: pl.BlockSpec((None,) + shape, lambda i: (layer, 0, 0),
                                       pipeline_mode=pl.Buffered(1))
    return pl.pallas_call(
        functools.partial(_ffn_kernel, final=final),
        grid=(t // tm,),
        in_specs=[
            pl.BlockSpec((tm, D_MODEL), lambda i: (i, 0)),
            pl.BlockSpec((None, 1, D_MODEL), lambda i: (layer, 0, 0)),
            wspec((D_MODEL, D_FF)),
            wspec((D_MODEL, D_FF)),
            wspec((D_FF, D_MODEL)),
            pl.BlockSpec((1, D_MODEL), lambda i: (0, 0)),
        ],
        out_specs=pl.BlockSpec((tm, D_MODEL), lambda i: (i, 0)),
        out_shape=jax.ShapeDtypeStruct((t, D_MODEL), F32),
        compiler_params=pltpu.CompilerParams(
            dimension_semantics=("arbitrary",), vmem_limit_bytes=VMEM_LIMIT),
        name="ffn",
    )(x2, nw, wg, wu, wd, fw)


def _rope_kernel(pos_ref, freq_ref, cos_ref, sin_ref):
    ang = pos_ref[...] * freq_ref[...]
    cos_ref[...] = jnp.cos(ang)
    sin_ref[...] = jnp.sin(ang)


def _rope_tables(pos_f32):
    t = pos_f32.shape[0]
    rows = min(1024, t)
    half = HEAD_DIM // 2
    freq = np.power(ROPE_BASE, -np.arange(half, dtype=np.float32) / half).astype(np.float32)
    freq = jnp.asarray(np.tile(freq, LANES // half)[None, :])
    return pl.pallas_call(
        _rope_kernel,
        grid=(t // rows,),
        in_specs=[pl.BlockSpec((rows, 1), lambda i: (i, 0)),
                  pl.BlockSpec((1, LANES), lambda i: (0, 0))],
        out_specs=[pl.BlockSpec((rows, LANES), lambda i: (i, 0)),
                   pl.BlockSpec((rows, LANES), lambda i: (i, 0))],
        out_shape=[jax.ShapeDtypeStruct((t, LANES), F32)] * 2,
        name="rope_tables",
    )(pos_f32, freq)


def _s5_re(j):
    return slice(2 * j * LANES, (2 * j + 1) * LANES)


def _s5_im(j):
    return slice((2 * j + 1) * LANES, (2 * j + 2) * LANES)


def _pow_rows(nsteps):
    return SUB * ((nsteps + SUB - 1) // SUB)


def _s5_prep_kernel(lr_ref, li_ref, ldt_ref, bre_ref, bim_ref, cre_ref, cim_ref,
                    bbar_ref, cmat_ref, tab_ref, *, nsteps):
    lr, li = lr_ref[...], li_ref[...]
    dt = jnp.exp(ldt_ref[...])
    mag = jnp.exp(lr * dt)
    ar, ai = mag * jnp.cos(li * dt), mag * jnp.sin(li * dt)
    den = lr * lr + li * li
    nr = ar - 1.0
    cr = (nr * lr + ai * li) / den
    ci = (ai * lr - nr * li) / den
    bre, bim = bre_ref[...], bim_ref[...]
    bbar_re = (cr * bre - ci * bim).astype(BF16)
    bbar_im = (cr * bim + ci * bre).astype(BF16)
    cre = cre_ref[...].astype(BF16)
    cim = (-cim_ref[...]).astype(BF16)
    for j in range(S5_NSTATE // LANES):
        src = slice(j * LANES, (j + 1) * LANES)
        bbar_ref[:, _s5_re(j)] = bbar_re[:, src]
        bbar_ref[:, _s5_im(j)] = bbar_im[:, src]
        cmat_ref[_s5_re(j), :] = cre[src, :]
        cmat_ref[_s5_im(j), :] = cim[src, :]

    def put(r, vr, vi):
        for j in range(S5_NSTATE // LANES):
            src = slice(j * LANES, (j + 1) * LANES)
            tab_ref[r:r + 1, _s5_re(j)] = vr[:, src]
            tab_ref[r:r + 1, _s5_im(j)] = vi[:, src]

    zero = jnp.zeros_like(ar)
    npow = _pow_rows(nsteps)
    pows = []
    pr, pi = ar, ai
    for k in range(npow):
        put(k, pr, pi)
        pows.append((pr, pi))
        pr, pi = pr * pr - pi * pi, 2.0 * pr * pi
    pr, pi = ar, ai
    for r in range(SUB):
        put(npow + r, pr, pi)
        pr, pi = pr * ar - pi * ai, pr * ai + pi * ar
    for k in range(3):
        for r in range(SUB):
            keep = r >= (1 << k)
            put(npow + SUB + k * SUB + r, pows[k][0] if keep else zero, pows[k][1] if keep else zero)


def _s5_prep(lr, li, ldt, bre, bim, cre, cim, nsteps):
    nl = lr.shape[0]
    nrows = _pow_rows(nsteps) + 4 * SUB
    vec = pl.BlockSpec((None, 1, S5_NSTATE), lambda l: (l, 0, 0))
    return pl.pallas_call(
        functools.partial(_s5_prep_kernel, nsteps=nsteps),
        grid=(nl,),
        in_specs=[vec, vec, vec,
                  pl.BlockSpec((None, S5_WIDTH, S5_NSTATE), lambda l: (l, 0, 0)),
                  pl.BlockSpec((None, S5_WIDTH, S5_NSTATE), lambda l: (l, 0, 0)),
                  pl.BlockSpec((None, S5_NSTATE, S5_WIDTH), lambda l: (l, 0, 0)),
                  pl.BlockSpec((None, S5_NSTATE, S5_WIDTH), lambda l: (l, 0, 0))],
        out_specs=[pl.BlockSpec((None, S5_WIDTH, 2 * S5_NSTATE), lambda l: (l, 0, 0)),
                   pl.BlockSpec((None, 2 * S5_NSTATE, S5_WIDTH), lambda l: (l, 0, 0)),
                   pl.BlockSpec((None, nrows, 2 * S5_NSTATE), lambda l: (l, 0, 0))],
        out_shape=[jax.ShapeDtypeStruct((nl, S5_WIDTH, 2 * S5_NSTATE), BF16),
                   jax.ShapeDtypeStruct((nl, 2 * S5_NSTATE, S5_WIDTH), BF16),
                   jax.ShapeDtypeStruct((nl, nrows, 2 * S5_NSTATE), F32)],
        name="s5_prep",
    )(lr, li, ldt, bre, bim, cre, cim)


def _mixer_kernel(
        x_ref, cos_ref, sin_ref, nw_ref, win_ref,
        gconv_ref, galog_ref, gdtb_ref, gnorm_ref,
        bbar_ref, cmat_ref, tab_ref, s5d_ref, wglu_ref, bglu_ref,
        lcw_ref, lcb_ref, lwa_ref, lba_ref, lwi_ref, lbi_ref, llam_ref,
        rnorm_ref, rintra_ref, rxi_ref, rzeta_ref, rdec_ref,
        wgate_ref, wbr_ref, wout_ref,
        o_ref,
        gbuf, lbuf, qkv_s, gcb_s, og_s, xs_s, ge_s, lg_s, gate_s, mrg_s, gstate, s5c, lruc, rstate,
        *, ts, nsteps):
    first = pl.program_id(1) == 0

    @pl.when(first)
    def _():
        gbuf[...] = jnp.zeros(gbuf.shape, F32)
        lbuf[...] = jnp.zeros(lbuf.shape, F32)
        gstate[...] = jnp.zeros(gstate.shape, F32)
        rstate[...] = jnp.zeros(rstate.shape, F32)
        s5c[...] = jnp.zeros(s5c.shape, F32)
        lruc[...] = jnp.zeros(lruc.shape, F32)

    hn = _rms(x_ref[...], nw_ref[...]).astype(BF16)

    def proj(c0, width):
        return jnp.dot(hn, win_ref[:, c0:c0 + width], preferred_element_type=F32)

    r2 = _iota2((BW, BW), 0)
    c2 = _iota2((BW, BW), 1)
    same_head = _head_of(r2) == _head_of(c2)
    ones_bd = jnp.where(same_head, 1.0, 0.0).astype(BF16)
    lane = _iota2((1, BW), 1)

    def head_sum(v, terms=1):
        return _dot_wide_lhs(v, ones_bd, terms)

    def causal_conv(cur, hist_ref, w_ref, taps):
        ext = jnp.concatenate([hist_ref[...], cur], axis=0)
        hist_ref[...] = cur[ts - HALO:ts, :]
        y = w_ref[taps - 1:taps, :] * cur
        for d in range(1, taps):
            y = y + w_ref[taps - 1 - d:taps - d, :] * pltpu.roll(ext, d, 0)[HALO:, :]
        return y

    def merge_gate_piece(n, c):
        w = D_MODEL // 2
        gate_s[n, :, c * w:(c + 1) * w] = _sigmoid(
            jnp.dot(hn, wgate_ref[:, n * D_MODEL + c * w:n * D_MODEL + (c + 1) * w],
                    preferred_element_type=F32))

    def merge_branch(n, out_n):
        term = gate_s[n] * jnp.dot(out_n.astype(BF16), wbr_ref[n], preferred_element_type=F32)
        if merged["started"]:
            mrg_s[...] = mrg_s[...] + term
        else:
            mrg_s[...] = term
            merged["started"] = True

    merged = {"started": False}

    def gdn_inputs():
        qkv = _silu(causal_conv(proj(C_QKV, 3 * BW), gbuf, gconv_ref, GDN_CONV))
        q, k_, v = qkv[:, 0:BW], qkv[:, BW:2 * BW], qkv[:, 2 * BW:3 * BW]
        qkv_s[:, 0:BW] = q * lax.rsqrt(head_sum(q * q) + 1e-6) * (HEAD_DIM ** -0.5)
        qkv_s[:, BW:2 * BW] = k_ * lax.rsqrt(head_sum(k_ * k_) + 1e-6)
        qkv_s[:, 2 * BW:3 * BW] = v
        ba = s5["zbu"][:, BW:BW + LANES]
        beta = _sigmoid(ba)
        g = -jnp.exp(galog_ref[...]) * _softplus(ba + gdtb_ref[...])
        rt = _iota2((ts, ts), 0)
        ct = _iota2((ts, ts), 1)
        mchunk = jnp.where((_head_of(rt) == _head_of(ct)) & (ct <= rt), 1.0, 0.0).astype(BF16)
        gc = _dot_wide_rhs(mchunk, g, 3)
        for h in range(HEADS):
            gcb_s[:, h * LANES:(h + 1) * LANES] = jnp.broadcast_to(gc[:, HEADS + h:HEADS + h + 1], (ts, LANES))
            gcb_s[:, (HEADS + h) * LANES:(HEADS + h + 1) * LANES] = jnp.broadcast_to(beta[:, h:h + 1], (ts, LANES))

    hmask = [jnp.where(_head_of(lane) == h, 1.0, 0.0) for h in range(HEADS)]
    bd_b = jnp.where(same_head, 1.0, 0.0).astype(BF16)
    rc = _iota2((GDN_CHUNK, BW), 0)
    jc = lax.bitwise_and(_iota2((GDN_CHUNK, BW), 1), HEAD_DIM - 1)
    incl = jc <= rc
    strict = jc < rc
    eye = jnp.where(jc == rc, 1.0, 0.0)
    low_lane = _iota2((GDN_CHUNK, LANES), 1) < HEAD_DIM

    def stack(m):
        mb = m.astype(BF16)
        return jnp.concatenate([mb] * HEADS, axis=0) * bd_b

    def per_head_lanes(blocks):
        return jnp.concatenate([jnp.where(low_lane, blocks[0], blocks[1]),
                                jnp.where(low_lane, blocks[2], blocks[3])], axis=1)

    nchunk = ts // GDN_CHUNK
    crow = [slice(n * GDN_CHUNK, (n + 1) * GDN_CHUNK) for n in range(nchunk)]
    nt_dims = (((1,), (1,)), ((), ()))
    chunks = [dict() for _ in range(nchunk)]

    def gdn_local(n):
        qc = qkv_s[crow[n], 0:BW]
        kc = qkv_s[crow[n], BW:2 * BW]
        vc = qkv_s[crow[n], 2 * BW:3 * BW]
        gh = [gcb_s[crow[n], h * LANES:(h + 1) * LANES] for h in range(HEADS)]
        bh = [gcb_s[crow[n], (HEADS + h) * LANES:(HEADS + h + 1) * LANES] for h in range(HEADS)]
        g64 = per_head_lanes(gh)
        b64 = per_head_lanes(bh)
        grow = jnp.concatenate(gh, axis=0).T[0:GDN_CHUNK, :]
        decay = jnp.where(incl, jnp.exp(jnp.where(incl, g64 - grow, 0.0)), 0.0)
        ks_b = stack(kc)
        kk = lax.dot_general(kc.astype(BF16), ks_b, nt_dims, preferred_element_type=F32)
        qk = lax.dot_general(qc.astype(BF16), ks_b, nt_dims, preferred_element_type=F32)
        low = jnp.where(strict, b64 * kk * decay, 0.0)
        e64 = jnp.exp(g64)
        glast = g64[GDN_CHUNK - 1:GDN_CHUNK, :]
        chunks[n].update(
            tinv=eye - low, p=low, p_bd=stack(low),
            rhs_v=stack(vc * b64), rhs_k=stack(kc * (b64 * e64)), attn_b=(qk * decay).astype(BF16),
            q_dec=(qc * e64).astype(BF16), k_dec=(kc * jnp.exp(glast - g64)).astype(BF16),
            cdec=jnp.exp(glast))

    def gdn_inverse_step(n):
        c = chunks[n]
        c["p"] = jnp.dot(c["p"].astype(BF16), c["p_bd"], preferred_element_type=F32)
        c["p_bd"] = stack(c["p"])
        c["tinv"] = c["tinv"] + jnp.dot(c["tinv"].astype(BF16), c["p_bd"], preferred_element_type=F32)

    def gdn_solve(n):
        c = chunks[n]
        tinv_b = c["tinv"].astype(BF16)
        c["u"] = jnp.dot(tinv_b, c["rhs_v"], preferred_element_type=F32)
        c["w"] = jnp.dot(tinv_b, c["rhs_k"], preferred_element_type=F32).astype(BF16)

    def gdn_state_step(n):
        c = chunks[n]
        st = gstate[...]
        st_b = st.astype(BF16)
        v_new = c["u"] - jnp.dot(c["w"], st_b, preferred_element_type=F32)
        kv = lax.dot_general(c["k_dec"], v_new.astype(BF16), (((0,), (0,)), ((), ())),
                             preferred_element_type=F32)
        gstate[...] = st * c["cdec"] + jnp.where(same_head, kv, 0.0)
        og_s[crow[n], :] = (jnp.dot(c["q_dec"], st_b, preferred_element_type=F32)
                            + jnp.dot(c["attn_b"], stack(v_new), preferred_element_type=F32))

    def gdn_output():
        og = og_s[...]
        og = og * lax.rsqrt(head_sum(og * og) * (1.0 / HEAD_DIM) + NORM_EPS) * gnorm_ref[...]
        merge_branch(0, og * _silu(s5["zbu"][:, 0:BW]))

    ngrp = ts // SUB
    npow = _pow_rows(nsteps)
    rowe = _iota2((ngrp, LANES), 0)
    s5 = {}

    def s5_inputs():
        s5["zbu"] = proj(C_Z, C_LX - C_Z)
        s5["u"] = s5["zbu"][:, C_S5 - C_Z:]
        s5["u_b"] = s5["u"].astype(BF16)

    def s5_slab(j, between=()):
        between = list(between)
        lr_, li_ = _s5_re(j), _s5_im(j)
        if j % 2 == 0:
            s5["bu"] = jnp.dot(s5["u_b"], bbar_ref[:, 2 * j * LANES:(2 * j + 4) * LANES],
                               preferred_element_type=F32)
        bu = s5["bu"][:, (j % 2) * 2 * LANES:(j % 2 + 1) * 2 * LANES]
        xr = bu[:, 0:LANES].reshape(ngrp, SUB, LANES)
        xi = bu[:, LANES:].reshape(ngrp, SUB, LANES)
        for s in range(3):
            m0 = npow + SUB + s * SUB
            mr, mi = tab_ref[m0:m0 + SUB, lr_][None], tab_ref[m0:m0 + SUB, li_][None]
            sr, si = pltpu.roll(xr, 1 << s, 1), pltpu.roll(xi, 1 << s, 1)
            xr, xi = xr + (mr * sr - mi * si), xi + (mr * si + mi * sr)
        if between:
            between.pop(0)()
        ge_s[2 * j] = xr.reshape(ts, LANES)
        ge_s[2 * j + 1] = xi.reshape(ts, LANES)
        er = ge_s[2 * j, pl.ds(SUB - 1, ngrp, stride=SUB), :]
        ei = ge_s[2 * j + 1, pl.ds(SUB - 1, ngrp, stride=SUB), :]
        cr, ci = s5c[0:1, lr_], s5c[0:1, li_]
        a8r, a8i = tab_ref[3:4, lr_], tab_ref[3:4, li_]
        er = er + jnp.where(rowe == 0, a8r * cr - a8i * ci, 0.0)
        ei = ei + jnp.where(rowe == 0, a8r * ci + a8i * cr, 0.0)
        for s in range(3, nsteps):
            d = 1 << (s - 3)
            pr, pi = tab_ref[s:s + 1, lr_], tab_ref[s:s + 1, li_]
            sr = jnp.where(rowe >= d, pltpu.roll(er, d, 0), 0.0)
            si = jnp.where(rowe >= d, pltpu.roll(ei, d, 0), 0.0)
            er, ei = er + (pr * sr - pi * si), ei + (pr * si + pi * sr)
        s5c[0:1, lr_] = er[ngrp - 1:ngrp, :]
        s5c[0:1, li_] = ei[ngrp - 1:ngrp, :]
        pr_ = jnp.where(rowe == 0, cr, pltpu.roll(er, 1, 0))
        pi_ = jnp.where(rowe == 0, ci, pltpu.roll(ei, 1, 0))
        tr, ti = tab_ref[npow:npow + SUB, lr_], tab_ref[npow:npow + SUB, li_]
        if between:
            between.pop(0)()
        for gi in range(ngrp):
            rs = slice(gi * SUB, (gi + 1) * SUB)
            cgr, cgi = pr_[gi:gi + 1, :], pi_[gi:gi + 1, :]
            xs_s[rs, lr_] = xr[gi] + (tr * cgr - ti * cgi)
            xs_s[rs, li_] = xi[gi] + (tr * cgi + ti * cgr)

    def s5_output():
        y5 = (_dot(xs_s[:, 0:S5_NSTATE], cmat_ref[0:S5_NSTATE, :])
              + _dot(xs_s[:, S5_NSTATE:], cmat_ref[S5_NSTATE:, :]) + s5d_ref[...] * s5["u"])
        y5 = _gelu_tanh(y5)
        y5 = y5 * _sigmoid(_dot(y5, wglu_ref[...]) + bglu_ref[...])
        merge_branch(1, y5)

    rsub = lax.broadcasted_iota(jnp.int32, (ngrp, SUB, BW), 1)
    rowg = _iota2((ngrp, BW), 0)
    lru = {}

    def lru_inputs():
        lxy = proj(C_LX, 2 * BW)
        lru["y"] = lxy[:, BW:]
        xc = causal_conv(lxy[:, 0:BW], lbuf, lcw_ref, LRU_CONV) + lcb_ref[...]
        rg = _sigmoid(_dot(xc, lwa_ref[...]) + lba_ref[...])
        ig = _sigmoid(_dot(xc, lwi_ref[...]) + lbi_ref[...])
        log_a = -LRU_C * rg * _softplus(-llam_ref[...])
        mult = jnp.sqrt(1.0 - jnp.exp(2.0 * log_a))
        lru["a"] = jnp.exp(log_a).reshape(ngrp, SUB, BW)
        lru["x"] = (xc * ig * mult).reshape(ngrp, SUB, BW)

    def lru_group_step(s):
        d = 1 << s
        sa = jnp.where(rsub >= d, pltpu.roll(lru["a"], d, 1), 1.0)
        sx = jnp.where(rsub >= d, pltpu.roll(lru["x"], d, 1), 0.0)
        lru["x"] = lru["x"] + lru["a"] * sx
        lru["a"] = lru["a"] * sa

    def lru_finish():
        a2, x2 = lru["a"].reshape(ts, BW), lru["x"].reshape(ts, BW)
        halves = BW // LANES
        for c in range(halves):
            lg_s[c] = a2[:, c * LANES:(c + 1) * LANES]
            lg_s[halves + c] = x2[:, c * LANES:(c + 1) * LANES]
        ends = pl.ds(SUB - 1, ngrp, stride=SUB)
        ea = jnp.concatenate([lg_s[c, ends, :] for c in range(halves)], axis=1)
        ex = jnp.concatenate([lg_s[halves + c, ends, :] for c in range(halves)], axis=1)
        carry = lruc[0:1, :]
        ex = ex + jnp.where(rowg == 0, ea * carry, 0.0)
        for s in range(nsteps - 3):
            d = 1 << s
            sa = jnp.where(rowg >= d, pltpu.roll(ea, d, 0), 1.0)
            sx = jnp.where(rowg >= d, pltpu.roll(ex, d, 0), 0.0)
            ex = ex + ea * sx
            ea = ea * sa
        lruc[0:1, :] = ex[ngrp - 1:ngrp, :]
        enter = jnp.where(rowg == 0, carry, pltpu.roll(ex, 1, 0))
        hx = jnp.concatenate(
            [lru["x"][gi] + lru["a"][gi] * enter[gi:gi + 1, :] for gi in range(ngrp)], axis=0)
        merge_branch(2, hx * _gelu_tanh(lru["y"]))

    ret = {}
    qmask = [jnp.where(_qhead_of(lane) == h, 1.0, 0.0) for h in range(HEADS)]

    def ret_inputs():
        cos, sin = cos_ref[...], sin_ref[...]

        def rope(t):
            t1, t2 = t[:, 0:LANES], t[:, LANES:]
            return jnp.concatenate([t1 * cos - t2 * sin, t1 * sin + t2 * cos], axis=1)

        qkvg = proj(C_RQ, 4 * BW)
        rq = rope(qkvg[:, 0:BW])
        rk = rope(qkvg[:, BW:2 * BW]) * (HEAD_DIM ** -0.5)
        rv_b = qkvg[:, 2 * BW:3 * BW].astype(BF16)
        ret.update(rq=rq, rk_b=rk.astype(BF16), rv_b=rv_b, g=qkvg[:, 3 * BW:],
                   o=_dot(rq * rxi_ref[...], rstate[...]))
        qv_same = _qhead_of(r2) == _head_of(c2)
        rstate[...] = rstate[...] * rdec_ref[...] + jnp.where(
            qv_same, _dot_tn(rk * rzeta_ref[...], rv_b), 0.0)

    def ret_head(h):
        sc = _dot_nt(ret["rq"] * qmask[h], ret["rk_b"]) * rintra_ref[h]
        ret["o"] = ret["o"] + hmask[h] * _dot(sc, ret["rv_b"])

    def ret_output():
        orr = ret["o"]
        mu = head_sum(orr, 2) * (1.0 / HEAD_DIM)
        cen = orr - mu
        var = head_sum(cen * cen) * (1.0 / HEAD_DIM)
        on = cen * lax.rsqrt(var + NORM_EPS) * rnorm_ref[...]
        merge_branch(3, _silu(ret["g"]) * on)

    P = functools.partial
    nslab = S5_NSTATE // LANES
    pieces = [P(merge_gate_piece, n, c) for n in range(N_BRANCH) for c in range(2)]
    per_slab = len(pieces) // nslab
    slab = [P(s5_slab, j, pieces[j * per_slab:(j + 1) * per_slab]) for j in range(nslab)]
    fill = (slab[0:4] + [lru_inputs] + slab[4:8] + [P(lru_group_step, s) for s in range(3)] + [ret_inputs]
            + [P(ret_head, h) for h in range(HEADS)] + [s5_output, lru_finish, ret_output])
    links = ([[P(gdn_inverse_step, n) for n in range(nchunk)] for _ in range(5)]
             + [[P(gdn_solve, n) for n in range(nchunk)]]
             + [[P(gdn_state_step, n)] for n in range(nchunk)])
    per_link = -(-len(fill) // len(links))
    s5_inputs()
    gdn_inputs()
    for n in range(nchunk):
        gdn_local(n)
    for link in links:
        for work in link:
            work()
        for work in fill[:per_link]:
            work()
        fill = fill[per_link:]
    for work in fill:
        work()
    gdn_output()
    o_ref[...] = x_ref[...] + jnp.dot(mrg_s[...].astype(BF16), wout_ref[...], preferred_element_type=F32)


def _retention_tables(ts):
    hh = np.arange(HEADS, dtype=np.float64)
    log_gamma = np.log1p(-np.exp2(-5.0 - hh))
    idx = np.arange(ts, dtype=np.float64)
    rel = idx[:, None] - idx[None, :]
    intra = np.where(rel >= 0, np.exp(np.where(rel >= 0, rel, 0.0)[None] * log_gamma[:, None, None]), 0.0)
    xi = np.exp((idx + 1.0)[None] * log_gamma[:, None])
    zeta = np.exp((ts - 1.0 - idx)[None] * log_gamma[:, None])
    cdec = np.exp(ts * log_gamma)
    qhead = (np.arange(BW) % LANES) // (HEAD_DIM // 2)
    vhead = np.arange(BW) // HEAD_DIM
    xi_q = xi[qhead].T
    zeta_q = zeta[qhead].T
    dec = np.where(qhead[:, None] == vhead[None, :], cdec[qhead][:, None], 0.0)
    f = lambda a: jnp.asarray(a.astype(np.float32))
    return f(intra), f(xi_q), f(zeta_q), f(dec)


def _mixer(x3, cos3, sin3, layer, p, ts):
    b, s, _ = x3.shape
    nsteps = int(math.log2(ts))
    assert 1 << nsteps == ts and s % ts == 0 and ts % GDN_CHUNK == 0
    rintra, rxi, rzeta, rdec = _retention_tables(ts)

    def lspec(shape, single=False):
        nd = len(shape)
        kw = {"pipeline_mode": pl.Buffered(1)} if single else {}
        return pl.BlockSpec((None,) + shape, lambda bi, si: (layer,) + (0,) * nd, **kw)

    def cspec(shape):
        nd = len(shape)
        return pl.BlockSpec(shape, lambda bi, si: (0,) * nd)

    tile = lambda w: pl.BlockSpec((None, ts, w), lambda bi, si: (bi, si, 0))
    in_specs = [
        tile(D_MODEL), tile(LANES), tile(LANES),
        lspec((1, D_MODEL)), lspec((D_MODEL, N_A), single=True),
        lspec((GDN_CONV, 3 * BW)), lspec((1, LANES)), lspec((1, LANES)), lspec((1, BW)),
        lspec((S5_WIDTH, 2 * S5_NSTATE)), lspec((2 * S5_NSTATE, S5_WIDTH)),
        lspec((p["s5_tab"].shape[1], 2 * S5_NSTATE)), lspec((1, BW)), lspec((BW, BW)), lspec((1, BW)),
        lspec((LRU_CONV, BW)), lspec((1, BW)), lspec((BW, BW)), lspec((1, BW)), lspec((BW, BW)),
        lspec((1, BW)), lspec((1, BW)),
        lspec((1, BW)), cspec((HEADS, ts, ts)), cspec((ts, BW)), cspec((ts, BW)), cspec((BW, BW)),
        lspec((D_MODEL, N_BRANCH * D_MODEL), single=True), lspec((N_BRANCH, BW, D_MODEL), single=True),
        lspec((D_MODEL, D_MODEL), single=True),
    ]
    scratch = [
        pltpu.VMEM((HALO, 3 * BW), F32),
        pltpu.VMEM((HALO, BW), F32),
        pltpu.VMEM((ts, 3 * BW), F32),
        pltpu.VMEM((ts, 2 * HEADS * LANES), F32),
        pltpu.VMEM((ts, BW), F32),
        pltpu.VMEM((ts, 2 * S5_NSTATE), F32),
        pltpu.VMEM((2 * S5_NSTATE // LANES, ts, LANES), F32),
        pltpu.VMEM((2 * BW // LANES, ts, LANES), F32),
        pltpu.VMEM((N_BRANCH, ts, D_MODEL), F32),
        pltpu.VMEM((ts, D_MODEL), F32),
        pltpu.VMEM((BW, BW), F32),
        pltpu.VMEM((HALO, 2 * S5_NSTATE), F32),
        pltpu.VMEM((HALO, BW), F32),
        pltpu.VMEM((BW, BW), F32),
    ]
    return pl.pallas_call(
        functools.partial(_mixer_kernel, ts=ts, nsteps=nsteps),
        grid=(b, s // ts),
        in_specs=in_specs,
        out_specs=pl.BlockSpec((None, ts, D_MODEL), lambda bi, si: (bi, si, 0)),
        out_shape=jax.ShapeDtypeStruct((b, s, D_MODEL), F32),
        scratch_shapes=scratch,
        compiler_params=pltpu.CompilerParams(
            dimension_semantics=("arbitrary", "arbitrary"), vmem_limit_bytes=VMEM_LIMIT),
        name="mixer",
    )(x3, cos3, sin3, p["mix_norm"], p["w_in_a"],
      p["gdn_conv_w"], p["gdn_a_log"], p["gdn_dt_bias"], p["gdn_norm_w"],
      p["s5_bbar"], p["s5_cmat"], p["s5_tab"], p["s5_d"], p["s5_w_glu"], p["s5_b_glu"],
      p["lru_conv_w"], p["lru_conv_b"], p["lru_w_a"], p["lru_b_a"], p["lru_w_i"], p["lru_b_i"],
      p["lru_lambda"], p["ret_norm_w"], rintra, rxi, rzeta, rdec,
      p["w_gate"], p["w_branch"], p["w_out"])


def _split_in_proj(w_in):
    offs = np.cumsum([0, 3 * BW, BW, HEADS, HEADS, BW, BW, BW, BW, BW, BW, BW])
    qkv, _, beta, _, s5, _, _, rq, rk, rv, _, gate0 = (int(o) for o in offs)
    nl, d, _ = w_in.shape

    def rope_split(w):
        return w.reshape(nl, d, HEADS, 2, HEAD_DIM // 2).transpose(0, 1, 3, 2, 4).reshape(nl, d, BW)

    ba = jnp.pad(w_in[:, :, beta:s5], ((0, 0), (0, 0), (0, LANES - 2 * HEADS)))
    w_a = jnp.concatenate([w_in[:, :, qkv:beta], ba, w_in[:, :, s5:rq], rope_split(w_in[:, :, rq:rk]),
                           rope_split(w_in[:, :, rk:rv]), w_in[:, :, rv:gate0]], axis=2)
    assert w_a.shape[2] == N_A
    return w_a, w_in[:, :, gate0:]


def _block_diag(blocks):
    nl, n, r, c = blocks.shape
    eye = jnp.eye(n, dtype=blocks.dtype)
    return (blocks[:, :, :, None, :] * eye[None, :, None, :, None]).reshape(nl, n * r, n * c)


def _prepare(prm, ts):
    nl = prm["w_in"].shape[0]
    row = lambda a: a.reshape(nl, 1, -1)
    rep = lambda a, n: jnp.repeat(a, n, axis=-1).reshape(nl, 1, -1)
    decay_lanes = lambda a: jnp.pad(a, ((0, 0), (HEADS, LANES - 2 * HEADS))).reshape(nl, 1, LANES)
    p = {}
    w_a, w_gate = _split_in_proj(prm["w_in"])
    p["w_in_a"] = w_a.astype(BF16)
    p["w_gate"] = w_gate.astype(BF16)
    p["mix_norm"] = row(prm["mix_norm"])
    p["gdn_conv_w"] = prm["gdn_conv_w"]
    p["gdn_a_log"] = decay_lanes(prm["gdn_a_log"])
    p["gdn_dt_bias"] = decay_lanes(prm["gdn_dt_bias"])
    p["gdn_norm_w"] = jnp.tile(prm["gdn_norm_w"], (1, HEADS)).reshape(nl, 1, BW)
    lr = row(prm["s5_lambda_re"])
    li = row(prm["s5_lambda_im"])
    ldt = rep(prm["s5_log_dt"], S5_STATE)
    bre = _block_diag(jnp.swapaxes(prm["s5_b_re"], 2, 3))
    bim = _block_diag(jnp.swapaxes(prm["s5_b_im"], 2, 3))
    cre = _block_diag(jnp.swapaxes(prm["s5_c_re"], 2, 3))
    cim = _block_diag(jnp.swapaxes(prm["s5_c_im"], 2, 3))
    p["s5_bbar"], p["s5_cmat"], p["s5_tab"] = _s5_prep(lr, li, ldt, bre, bim, cre, cim, int(math.log2(ts)))
    p["s5_d"] = row(prm["s5_d"])
    p["s5_w_glu"] = prm["s5_w_glu"].astype(BF16)
    p["s5_b_glu"] = row(prm["s5_b_glu"])
    p["lru_conv_w"] = prm["lru_conv_w"]
    p["lru_conv_b"] = row(prm["lru_conv_b"])
    p["lru_w_a"] = _block_diag(prm["lru_w_a"]).astype(BF16)
    p["lru_w_i"] = _block_diag(prm["lru_w_i"]).astype(BF16)
    p["lru_b_a"] = row(prm["lru_b_a"])
    p["lru_b_i"] = row(prm["lru_b_i"])
    p["lru_lambda"] = row(prm["lru_lambda"])
    p["ret_norm_w"] = row(prm["ret_norm_w"])
    p["w_branch"] = prm["w_branch"].astype(BF16)
    p["w_out"] = prm["w_out"].astype(BF16)
    for f in ("ffn1", "ffn2"):
        p[f + "_norm"] = row(prm[f + "_norm"])
        for w in ("w_gate", "w_up", "w_down"):
            p[f + "_" + w] = prm[f + "_" + w].astype(BF16)
    return p


def _trunk(x, positions, prm, final_norm, ts):
    b, s, d = x.shape
    nl = prm["w_in"].shape[0]
    p = _prepare(prm, ts)
    cos, sin = _rope_tables(positions.astype(F32).reshape(b * s, 1))
    cos3, sin3 = cos.reshape(b, s, LANES), sin.reshape(b, s, LANES)
    fw = final_norm.reshape(1, d)
    x2 = x.reshape(b * s, d)
    for l in range(nl):
        x2 = _ffn(x2, l, p["ffn1_norm"], p["ffn1_w_gate"], p["ffn1_w_up"], p["ffn1_w_down"], fw, False)
        x2 = _mixer(x2.reshape(b, s, d), cos3, sin3, l, p, ts).reshape(b * s, d)
        x2 = _ffn(x2, l, p["ffn2_norm"], p["ffn2_w_gate"], p["ffn2_w_up"], p["ffn2_w_down"], fw,
                  l == nl - 1)
    return x2.reshape(b, s, d)


def kernel(x, positions, ffn1_norm, ffn1_w_gate, ffn1_w_up, ffn1_w_down, mix_norm, w_in, gdn_conv_w, gdn_a_log, gdn_dt_bias, gdn_norm_w, s5_lambda_re, s5_lambda_im, s5_b_re, s5_b_im, s5_c_re, s5_c_im, s5_d, s5_log_dt, s5_w_glu, s5_b_glu, lru_conv_w, lru_conv_b, lru_w_a, lru_b_a, lru_w_i, lru_b_i, lru_lambda, ret_norm_w, w_branch, w_out, ffn2_norm, ffn2_w_gate, ffn2_w_up, ffn2_w_down, final_norm):
    prm = dict(
        ffn1_norm=ffn1_norm, ffn1_w_gate=ffn1_w_gate, ffn1_w_up=ffn1_w_up, ffn1_w_down=ffn1_w_down,
        mix_norm=mix_norm, w_in=w_in, gdn_conv_w=gdn_conv_w, gdn_a_log=gdn_a_log,
        gdn_dt_bias=gdn_dt_bias, gdn_norm_w=gdn_norm_w, s5_lambda_re=s5_lambda_re,
        s5_lambda_im=s5_lambda_im, s5_b_re=s5_b_re, s5_b_im=s5_b_im, s5_c_re=s5_c_re, s5_c_im=s5_c_im,
        s5_d=s5_d, s5_log_dt=s5_log_dt, s5_w_glu=s5_w_glu, s5_b_glu=s5_b_glu, lru_conv_w=lru_conv_w,
        lru_conv_b=lru_conv_b, lru_w_a=lru_w_a, lru_b_a=lru_b_a, lru_w_i=lru_w_i, lru_b_i=lru_b_i,
        lru_lambda=lru_lambda, ret_norm_w=ret_norm_w, w_branch=w_branch, w_out=w_out,
        ffn2_norm=ffn2_norm, ffn2_w_gate=ffn2_w_gate, ffn2_w_up=ffn2_w_up, ffn2_w_down=ffn2_w_down)
    return _trunk(x, positions, prm, final_norm, SEQ_TILE)
```

```python
import functools
import math

import numpy as np
import jax
import jax.numpy as jnp
from jax import lax
from jax.experimental import pallas as pl
from jax.experimental.pallas import tpu as pltpu

F32 = jnp.float32
BF16 = jnp.bfloat16

D_MODEL = 1024
D_FF = 2816
NORM_EPS = 1e-6
N_BRANCH = 4
HEADS = 4
HEAD_DIM = 64
GDN_CONV = 4
GDN_CHUNK = 64
S5_GROUPS = 16
S5_GROUP_WIDTH = 16
S5_STATE = 64
S5_WIDTH = S5_GROUPS * S5_GROUP_WIDTH
S5_NSTATE = S5_GROUPS * S5_STATE
LRU_WIDTH = 256
LRU_BLOCKS = 4
LRU_CONV = 4
LRU_C = 8.0
ROPE_BASE = 10000.0
BW = 256

LANES = 128
SUB = 8
HALO = SUB

SEQ_TILE = 256
TOK_TILE = 1024
FF_CHUNK = 256
VMEM_LIMIT = 56 * 1024 * 1024

C_QKV = 0
C_Z = C_QKV + 3 * BW
C_BA = C_Z + BW
C_S5 = C_BA + LANES
C_LX = C_S5 + BW
C_LY = C_LX + BW
C_RQ = C_LY + BW
C_RK = C_RQ + BW
C_RV = C_RK + BW
C_RG = C_RV + BW
N_A = C_RG + BW


def _sigmoid(x):
    return 0.5 + 0.5 * jnp.tanh(0.5 * x)


def _silu(x):
    return x * _sigmoid(x)


def _gelu_tanh(x):
    return 0.5 * x * (1.0 + jnp.tanh(math.sqrt(2.0 / math.pi) * (x + 0.044715 * (x * x * x))))


def _softplus(x):
    return jnp.maximum(x, 0.0) + jnp.log(1.0 + jnp.exp(-jnp.abs(x)))


def _rms(x, w):
    return x * lax.rsqrt(jnp.mean(x * x, axis=-1, keepdims=True) + NORM_EPS) * w


def _dot(a, b):
    return jnp.dot(a.astype(BF16), b.astype(BF16), preferred_element_type=F32)


def _dot_nt(a, b):
    return lax.dot_general(a.astype(BF16), b.astype(BF16), (((1,), (1,)), ((), ())),
                           preferred_element_type=F32)


def _dot_tn(a, b):
    return lax.dot_general(a.astype(BF16), b.astype(BF16), (((0,), (0,)), ((), ())),
                           preferred_element_type=F32)


def _split_bf16(x, n):
    terms = []
    for _ in range(n - 1):
        t = x.astype(BF16)
        terms.append(t)
        x = x - t.astype(F32)
    terms.append(x.astype(BF16))
    return terms


def _dot_wide_rhs(m_bf16, x, n):
    return sum(jnp.dot(m_bf16, t, preferred_element_type=F32) for t in _split_bf16(x, n))


def _dot_wide_lhs(x, m_bf16, n):
    return sum(jnp.dot(t, m_bf16, preferred_element_type=F32) for t in _split_bf16(x, n))


def _iota2(shape, axis):
    return lax.broadcasted_iota(jnp.int32, shape, axis)


def _head_of(i):
    return lax.shift_right_logical(i, 6)


def _qhead_of(i):
    return lax.shift_right_logical(lax.bitwise_and(i, LANES - 1), 5)


def _ffn_kernel(x_ref, nw_ref, wg_ref, wu_ref, wd_ref, fw_ref, o_ref, *, final):
    x = x_ref[...]
    hn = _rms(x, nw_ref[...]).astype(BF16)
    acc = jnp.zeros(x.shape, F32)
    for c in range(D_FF // FF_CHUNK):
        cs = slice(c * FF_CHUNK, (c + 1) * FF_CHUNK)
        g = jnp.dot(hn, wg_ref[:, cs], preferred_element_type=F32)
        u = jnp.dot(hn, wu_ref[:, cs], preferred_element_type=F32)
        a = (_silu(g) * u).astype(BF16)
        acc = acc + jnp.dot(a, wd_ref[cs, :], preferred_element_type=F32)
    y = x + 0.5 * acc
    if final:
        y = _rms(y, fw_ref[...])
    o_ref[...] = y


def _ffn(x2, layer, nw, wg, wu, wd, fw, final):
    t = x2.shape[0]
    tm = min(TOK_TILE, t)
    wspec = lambda shape: pl.BlockSpec((None,) + shape, lambda i: (layer, 0, 0),
                                       pipeline_mode=pl.Buffered(1))
    return pl.pallas_call(
        functools.partial(_ffn_kernel, final=final),
        grid=(t // tm,),
        in_specs=[
            pl.BlockSpec((tm, D_MODEL), lambda i: (i, 0)),
            pl.BlockSpec((None, 1, D_MODEL), lambda i: (layer, 0, 0)),
            wspec((D_MODEL, D_FF)),
            wspec((D_MODEL, D_FF)),
            wspec((D_FF, D_MODEL)),
            pl.BlockSpec((1, D_MODEL), lambda i: (0, 0)),
        ],
        out_specs=pl.BlockSpec((tm, D_MODEL), lambda i: (i, 0)),
        out_shape=jax.ShapeDtypeStruct((t, D_MODEL), F32),
        compiler_params=pltpu.CompilerParams(
            dimension_semantics=("arbitrary",), vmem_limit_bytes=VMEM_LIMIT),
        name="ffn",
    )(x2, nw, wg, wu, wd, fw)


def _rope_kernel(pos_ref, freq_ref, cos_ref, sin_ref):
    ang = pos_ref[...] * freq_ref[...]
    cos_ref[...] = jnp.cos(ang)
    sin_ref[...] = jnp.sin(ang)


def _rope_tables(pos_f32):
    t = pos_f32.shape[0]
    rows = min(1024, t)
    half = HEAD_DIM // 2
    freq = np.power(ROPE_BASE, -np.arange(half, dtype=np.float32) / half).astype(np.float32)
    freq = jnp.asarray(np.tile(freq, LANES // half)[None, :])
    return pl.pallas_call(
        _rope_kernel,
        grid=(t // rows,),
        in_specs=[pl.BlockSpec((rows, 1), lambda i: (i, 0)),
                  pl.BlockSpec((1, LANES), lambda i: (0, 0))],
        out_specs=[pl.BlockSpec((rows, LANES), lambda i: (i, 0)),
                   pl.BlockSpec((rows, LANES), lambda i: (i, 0))],
        out_shape=[jax.ShapeDtypeStruct((t, LANES), F32)] * 2,
        name="rope_tables",
    )(pos_f32, freq)


def _s5_re(j):
    return slice(2 * j * LANES, (2 * j + 1) * LANES)


def _s5_im(j):
    return slice((2 * j + 1) * LANES, (2 * j + 2) * LANES)


def _pow_rows(nsteps):
    return SUB * ((nsteps + SUB - 1) // SUB)


def _s5_prep_kernel(lr_ref, li_ref, ldt_ref, bre_ref, bim_ref, cre_ref, cim_ref,
                    bbar_ref, cmat_ref, tab_ref, *, nsteps):
    lr, li = lr_ref[...], li_ref[...]
    dt = jnp.exp(ldt_ref[...])
    mag = jnp.exp(lr * dt)
    ar, ai = mag * jnp.cos(li * dt), mag * jnp.sin(li * dt)
    den = lr * lr + li * li
    nr = ar - 1.0
    cr = (nr * lr + ai * li) / den
    ci = (ai * lr - nr * li) / den
    bre, bim = bre_ref[...], bim_ref[...]
    bbar_re = (cr * bre - ci * bim).astype(BF16)
    bbar_im = (cr * bim + ci * bre).astype(BF16)
    cre = cre_ref[...].astype(BF16)
    cim = (-cim_ref[...]).astype(BF16)
    for j in range(S5_NSTATE // LANES):
        src = slice(j * LANES, (j + 1) * LANES)
        bbar_ref[:, _s5_re(j)] = bbar_re[:, src]
        bbar_ref[:, _s5_im(j)] = bbar_im[:, src]
        cmat_ref[_s5_re(j), :] = cre[src, :]
        cmat_ref[_s5_im(j), :] = cim[src, :]

    def put(r, vr, vi):
        for j in range(S5_NSTATE // LANES):
            src = slice(j * LANES, (j + 1) * LANES)
            tab_ref[r:r + 1, _s5_re(j)] = vr[:, src]
            tab_ref[r:r + 1, _s5_im(j)] = vi[:, src]

    zero = jnp.zeros_like(ar)
    npow = _pow_rows(nsteps)
    pows = []
    pr, pi = ar, ai
    for k in range(npow):
        put(k, pr, pi)
        pows.append((pr, pi))
        pr, pi = pr * pr - pi * pi, 2.0 * pr * pi
    pr, pi = ar, ai
    for r in range(SUB):
        put(npow + r, pr, pi)
        pr, pi = pr * ar - pi * ai, pr * ai + pi * ar
    for k in range(3):
        for r in range(SUB):
            keep = r >= (1 << k)
            put(npow + SUB + k * SUB + r, pows[k][0] if keep else zero, pows[k][1] if keep else zero)


def _s5_prep(lr, li, ldt, bre, bim, cre, cim, nsteps):
    nl = lr.shape[0]
    nrows = _pow_rows(nsteps) + 4 * SUB
    vec = pl.BlockSpec((None, 1, S5_NSTATE), lambda l: (l, 0, 0))
    return pl.pallas_call(
        functools.partial(_s5_prep_kernel, nsteps=nsteps),
        grid=(nl,),
        in_specs=[vec, vec, vec,
                  pl.BlockSpec((None, S5_WIDTH, S5_NSTATE), lambda l: (l, 0, 0)),
                  pl.BlockSpec((None, S5_WIDTH, S5_NSTATE), lambda l: (l, 0, 0)),
                  pl.BlockSpec((None, S5_NSTATE, S5_WIDTH), lambda l: (l, 0, 0)),
                  pl.BlockSpec((None, S5_NSTATE, S5_WIDTH), lambda l: (l, 0, 0))],
        out_specs=[pl.BlockSpec((None, S5_WIDTH, 2 * S5_NSTATE), lambda l: (l, 0, 0)),
                   pl.BlockSpec((None, 2 * S5_NSTATE, S5_WIDTH), lambda l: (l, 0, 0)),
                   pl.BlockSpec((None, nrows, 2 * S5_NSTATE), lambda l: (l, 0, 0))],
        out_shape=[jax.ShapeDtypeStruct((nl, S5_WIDTH, 2 * S5_NSTATE), BF16),
                   jax.ShapeDtypeStruct((nl, 2 * S5_NSTATE, S5_WIDTH), BF16),
                   jax.ShapeDtypeStruct((nl, nrows, 2 * S5_NSTATE), F32)],
        name="s5_prep",
    )(lr, li, ldt, bre, bim, cre, cim)


def _mixer_kernel(
        x_ref, cos_ref, sin_ref, nw_ref, win_ref,
        gconv_ref, galog_ref, gdtb_ref, gnorm_ref,
        bbar_ref, cmat_ref, tab_ref, s5d_ref, wglu_ref, bglu_ref,
        lcw_ref, lcb_ref, lwai_ref, lbai_ref, llam_ref,
        rnorm_ref, rintra_ref, rxi_ref, rzeta_ref, rdec_ref,
        wgate_ref, wbr_ref, wout_ref,
        o_ref,
        gbuf, lbuf, qkv_s, gcb_s, og_s, xs_s, ge_s, lg_s, gate_s, mrg_s, gstate, s5c, lruc, rstate,
        *, ts, nsteps):
    first = pl.program_id(1) == 0

    @pl.when(first)
    def _():
        gbuf[...] = jnp.zeros(gbuf.shape, F32)
        lbuf[...] = jnp.zeros(lbuf.shape, F32)
        gstate[...] = jnp.zeros(gstate.shape, F32)
        rstate[...] = jnp.zeros(rstate.shape, F32)
        s5c[...] = jnp.zeros(s5c.shape, F32)
        lruc[...] = jnp.zeros(lruc.shape, F32)

    hn = _rms(x_ref[...], nw_ref[...]).astype(BF16)
    nt_dims = (((1,), (1,)), ((), ()))

    def proj(c0, width):
        return jnp.dot(hn, win_ref[:, c0:c0 + width], preferred_element_type=F32)

    r2 = _iota2((BW, BW), 0)
    c2 = _iota2((BW, BW), 1)
    same_head = _head_of(r2) == _head_of(c2)
    ones_bd = jnp.where(same_head, 1.0, 0.0).astype(BF16)
    lane = _iota2((1, BW), 1)

    def head_sum(v, terms=1):
        return _dot_wide_lhs(v, ones_bd, terms)

    def causal_conv(cur, hist_ref, w_ref, taps):
        ext = jnp.concatenate([hist_ref[...], cur], axis=0)
        hist_ref[...] = cur[ts - HALO:ts, :]
        y = w_ref[taps - 1:taps, :] * cur
        for d in range(1, taps):
            y = y + w_ref[taps - 1 - d:taps - d, :] * pltpu.roll(ext, d, 0)[HALO:, :]
        return y

    def merge_gate_piece(n, c):
        w = D_MODEL // 2
        gate_s[n, c * w:(c + 1) * w, :] = _sigmoid(
            lax.dot_general(wgate_ref[n, c * w:(c + 1) * w, :], hn, nt_dims, preferred_element_type=F32))

    def merge_branch(n, out_n):
        term = gate_s[n] * lax.dot_general(wbr_ref[n], out_n.astype(BF16), nt_dims,
                                           preferred_element_type=F32)
        if merged["started"]:
            mrg_s[...] = mrg_s[...] + term
        else:
            mrg_s[...] = term
            merged["started"] = True

    merged = {"started": False}

    def gdn_inputs():
        qkv = _silu(causal_conv(proj(C_QKV, 3 * BW), gbuf, gconv_ref, GDN_CONV))
        q, k_, v = qkv[:, 0:BW], qkv[:, BW:2 * BW], qkv[:, 2 * BW:3 * BW]
        ss = head_sum(jnp.concatenate([q * q, k_ * k_], axis=0))
        qkv_s[:, 0:BW] = q * lax.rsqrt(ss[0:ts] + 1e-6) * (HEAD_DIM ** -0.5)
        qkv_s[:, BW:2 * BW] = k_ * lax.rsqrt(ss[ts:] + 1e-6)
        qkv_s[:, 2 * BW:3 * BW] = v
        ba = s5["zbu"][:, BW:BW + LANES]
        beta = _sigmoid(ba)
        g = -jnp.exp(galog_ref[...]) * _softplus(ba + gdtb_ref[...])
        rt = _iota2((ts, ts), 0)
        ct = _iota2((ts, ts), 1)
        mchunk = jnp.where((_head_of(rt) == _head_of(ct)) & (ct <= rt), 1.0, 0.0).astype(BF16)
        gc = _dot_wide_rhs(mchunk, g, 3)
        for h in range(HEADS):
            gcb_s[:, h * LANES:(h + 1) * LANES] = jnp.broadcast_to(gc[:, HEADS + h:HEADS + h + 1], (ts, LANES))
            gcb_s[:, (HEADS + h) * LANES:(HEADS + h + 1) * LANES] = jnp.broadcast_to(beta[:, h:h + 1], (ts, LANES))

    hmask = [jnp.where(_head_of(lane) == h, 1.0, 0.0) for h in range(HEADS)]
    bd_b = jnp.where(same_head, 1.0, 0.0).astype(BF16)
    rc = _iota2((GDN_CHUNK, BW), 0)
    jc = lax.bitwise_and(_iota2((GDN_CHUNK, BW), 1), HEAD_DIM - 1)
    incl = jc <= rc
    strict = jc < rc
    eye = jnp.where(jc == rc, 1.0, 0.0)
    low_lane = _iota2((GDN_CHUNK, LANES), 1) < HEAD_DIM

    def stack(m):
        mb = m.astype(BF16)
        return jnp.concatenate([mb] * HEADS, axis=0) * bd_b

    def per_head_lanes(blocks):
        return jnp.concatenate([jnp.where(low_lane, blocks[0], blocks[1]),
                                jnp.where(low_lane, blocks[2], blocks[3])], axis=1)

    nchunk = ts // GDN_CHUNK
    crow = [slice(n * GDN_CHUNK, (n + 1) * GDN_CHUNK) for n in range(nchunk)]
    chunks = [dict() for _ in range(nchunk)]

    def gdn_local(n):
        qc = qkv_s[crow[n], 0:BW]
        kc = qkv_s[crow[n], BW:2 * BW]
        vc = qkv_s[crow[n], 2 * BW:3 * BW]
        gh = [gcb_s[crow[n], h * LANES:(h + 1) * LANES] for h in range(HEADS)]
        bh = [gcb_s[crow[n], (HEADS + h) * LANES:(HEADS + h + 1) * LANES] for h in range(HEADS)]
        g64 = per_head_lanes(gh)
        b64 = per_head_lanes(bh)
        grow = jnp.concatenate(gh, axis=0).T[0:GDN_CHUNK, :]
        decay = jnp.where(incl, jnp.exp(jnp.where(incl, g64 - grow, 0.0)), 0.0)
        ks_b = stack(kc)
        kq = lax.dot_general(jnp.concatenate([kc, qc], axis=0).astype(BF16), ks_b, nt_dims,
                             preferred_element_type=F32)
        kk, qk = kq[0:GDN_CHUNK], kq[GDN_CHUNK:]
        low = jnp.where(strict, b64 * kk * decay, 0.0)
        e64 = jnp.exp(g64)
        glast = g64[GDN_CHUNK - 1:GDN_CHUNK, :]
        chunks[n].update(
            tinv=eye - low, p=low, p_bd=stack(low),
            rhs=jnp.concatenate([stack(vc * b64), stack(kc * (b64 * e64))], axis=1),
            attn_b=(qk * decay).astype(BF16),
            q_dec=(qc * e64).astype(BF16), k_dec=(kc * jnp.exp(glast - g64)).astype(BF16),
            cdec=jnp.exp(glast))

    def gdn_inverse_first(n):
        c = chunks[n]
        c["p"] = jnp.dot(c["p"].astype(BF16), c["p_bd"], preferred_element_type=F32)
        c["p_bd"] = stack(c["p"])

    def gdn_inverse_step(n, last):
        c = chunks[n]
        if last:
            c["tinv"] = c["tinv"] + jnp.dot(c["tinv"].astype(BF16), c["p_bd"], preferred_element_type=F32)
            return
        both = jnp.dot(jnp.concatenate([c["p"], c["tinv"]], axis=0).astype(BF16), c["p_bd"],
                       preferred_element_type=F32)
        c["tinv"] = c["tinv"] + both[GDN_CHUNK:]
        c["p"] = both[0:GDN_CHUNK]
        c["p_bd"] = stack(c["p"])

    def gdn_solve(n):
        c = chunks[n]
        tinv_b = c["tinv"].astype(BF16)
        uw = jnp.dot(tinv_b, c["rhs"], preferred_element_type=F32)
        c["u"] = uw[:, 0:BW]
        c["wq"] = jnp.concatenate([uw[:, BW:].astype(BF16), c["q_dec"]], axis=0)

    def gdn_state_step(n):
        c = chunks[n]
        st = gstate[...]
        st_b = st.astype(BF16)
        ws = jnp.dot(c["wq"], st_b, preferred_element_type=F32)
        v_new = c["u"] - ws[0:GDN_CHUNK]
        kv = lax.dot_general(c["k_dec"], v_new.astype(BF16), (((0,), (0,)), ((), ())),
                             preferred_element_type=F32)
        gstate[...] = st * c["cdec"] + jnp.where(same_head, kv, 0.0)
        og_s[crow[n], :] = ws[GDN_CHUNK:] + jnp.dot(c["attn_b"], stack(v_new), preferred_element_type=F32)

    def gdn_output():
        og = og_s[...]
        og = og * lax.rsqrt(head_sum(og * og) * (1.0 / HEAD_DIM) + NORM_EPS) * gnorm_ref[...]
        merge_branch(0, og * _silu(s5["zbu"][:, 0:BW]))

    ngrp = ts // SUB
    npow = _pow_rows(nsteps)
    rowe = _iota2((ngrp, LANES), 0)
    s5 = {}

    def s5_inputs():
        s5["zbu"] = proj(C_Z, C_LX - C_Z)
        s5["u"] = s5["zbu"][:, C_S5 - C_Z:]
        s5["u_b"] = s5["u"].astype(BF16)

    def s5_slab(j, between=()):
        between = list(between)
        if between:
            between.pop(0)()
        lr_, li_ = _s5_re(j), _s5_im(j)
        if j % 2 == 0:
            s5["bu"] = jnp.dot(s5["u_b"], bbar_ref[:, 2 * j * LANES:(2 * j + 4) * LANES],
                               preferred_element_type=F32)
        bu = s5["bu"][:, (j % 2) * 2 * LANES:(j % 2 + 1) * 2 * LANES]
        xr = bu[:, 0:LANES].reshape(ngrp, SUB, LANES)
        xi = bu[:, LANES:].reshape(ngrp, SUB, LANES)
        for s in range(3):
            m0 = npow + SUB + s * SUB
            mr, mi = tab_ref[m0:m0 + SUB, lr_][None], tab_ref[m0:m0 + SUB, li_][None]
            sr, si = pltpu.roll(xr, 1 << s, 1), pltpu.roll(xi, 1 << s, 1)
            xr, xi = xr + (mr * sr - mi * si), xi + (mr * si + mi * sr)
        if between:
            between.pop(0)()
        ge_s[2 * j] = xr.reshape(ts, LANES)
        ge_s[2 * j + 1] = xi.reshape(ts, LANES)
        er = ge_s[2 * j, pl.ds(SUB - 1, ngrp, stride=SUB), :]
        ei = ge_s[2 * j + 1, pl.ds(SUB - 1, ngrp, stride=SUB), :]
        cr, ci = s5c[0:1, lr_], s5c[0:1, li_]
        a8r, a8i = tab_ref[3:4, lr_], tab_ref[3:4, li_]
        er = er + jnp.where(rowe == 0, a8r * cr - a8i * ci, 0.0)
        ei = ei + jnp.where(rowe == 0, a8r * ci + a8i * cr, 0.0)
        for s in range(3, nsteps):
            d = 1 << (s - 3)
            pr, pi = tab_ref[s:s + 1, lr_], tab_ref[s:s + 1, li_]
            sr = jnp.where(rowe >= d, pltpu.roll(er, d, 0), 0.0)
            si = jnp.where(rowe >= d, pltpu.roll(ei, d, 0), 0.0)
            er, ei = er + (pr * sr - pi * si), ei + (pr * si + pi * sr)
        s5c[0:1, lr_] = er[ngrp - 1:ngrp, :]
        s5c[0:1, li_] = ei[ngrp - 1:ngrp, :]
        pr_ = jnp.where(rowe == 0, cr, pltpu.roll(er, 1, 0))
        pi_ = jnp.where(rowe == 0, ci, pltpu.roll(ei, 1, 0))
        tr, ti = tab_ref[npow:npow + SUB, lr_], tab_ref[npow:npow + SUB, li_]
        if between:
            between.pop(0)()
        for gi in range(ngrp):
            rs = slice(gi * SUB, (gi + 1) * SUB)
            cgr, cgi = pr_[gi:gi + 1, :], pi_[gi:gi + 1, :]
            xs_s[rs, lr_] = xr[gi] + (tr * cgr - ti * cgi)
            xs_s[rs, li_] = xi[gi] + (tr * cgi + ti * cgr)

    def s5_output():
        y5 = (_dot(xs_s[:, 0:S5_NSTATE], cmat_ref[0:S5_NSTATE, :])
              + _dot(xs_s[:, S5_NSTATE:], cmat_ref[S5_NSTATE:, :]) + s5d_ref[...] * s5["u"])
        y5 = _gelu_tanh(y5)
        y5 = y5 * _sigmoid(_dot(y5, wglu_ref[...]) + bglu_ref[...])
        merge_branch(1, y5)

    rsub = lax.broadcasted_iota(jnp.int32, (ngrp, SUB, BW), 1)
    rowg = _iota2((ngrp, BW), 0)
    lru = {}

    def lru_inputs():
        lxy = proj(C_LX, 2 * BW)
        lru["y"] = lxy[:, BW:]
        xc = causal_conv(lxy[:, 0:BW], lbuf, lcw_ref, LRU_CONV) + lcb_ref[...]
        gates = _sigmoid(_dot(xc, lwai_ref[...]) + lbai_ref[...])
        rg, ig = gates[:, 0:BW], gates[:, BW:]
        log_a = -LRU_C * rg * _softplus(-llam_ref[...])
        mult = jnp.sqrt(1.0 - jnp.exp(2.0 * log_a))
        lru["a"] = jnp.exp(log_a).reshape(ngrp, SUB, BW)
        lru["x"] = (xc * ig * mult).reshape(ngrp, SUB, BW)

    def lru_group_step(s):
        d = 1 << s
        sa = jnp.where(rsub >= d, pltpu.roll(lru["a"], d, 1), 1.0)
        sx = jnp.where(rsub >= d, pltpu.roll(lru["x"], d, 1), 0.0)
        lru["x"] = lru["x"] + lru["a"] * sx
        lru["a"] = lru["a"] * sa

    def lru_finish():
        a2, x2 = lru["a"].reshape(ts, BW), lru["x"].reshape(ts, BW)
        halves = BW // LANES
        for c in range(halves):
            lg_s[c] = a2[:, c * LANES:(c + 1) * LANES]
            lg_s[halves + c] = x2[:, c * LANES:(c + 1) * LANES]
        ends = pl.ds(SUB - 1, ngrp, stride=SUB)
        ea = jnp.concatenate([lg_s[c, ends, :] for c in range(halves)], axis=1)
        ex = jnp.concatenate([lg_s[halves + c, ends, :] for c in range(halves)], axis=1)
        carry = lruc[0:1, :]
        ex = ex + jnp.where(rowg == 0, ea * carry, 0.0)
        for s in range(nsteps - 3):
            d = 1 << s
            sa = jnp.where(rowg >= d, pltpu.roll(ea, d, 0), 1.0)
            sx = jnp.where(rowg >= d, pltpu.roll(ex, d, 0), 0.0)
            ex = ex + ea * sx
            ea = ea * sa
        lruc[0:1, :] = ex[ngrp - 1:ngrp, :]
        enter = jnp.where(rowg == 0, carry, pltpu.roll(ex, 1, 0))
        hx = jnp.concatenate(
            [lru["x"][gi] + lru["a"][gi] * enter[gi:gi + 1, :] for gi in range(ngrp)], axis=0)
        merge_branch(2, hx * _gelu_tanh(lru["y"]))

    ret = {}
    qmask = [jnp.where(_qhead_of(lane) == h, 1.0, 0.0) for h in range(HEADS)]

    def ret_inputs():
        cos, sin = cos_ref[...], sin_ref[...]

        def rope(t):
            t1, t2 = t[:, 0:LANES], t[:, LANES:]
            return jnp.concatenate([t1 * cos - t2 * sin, t1 * sin + t2 * cos], axis=1)

        qkvg = proj(C_RQ, 4 * BW)
        rq = rope(qkvg[:, 0:BW])
        rk = rope(qkvg[:, BW:2 * BW]) * (HEAD_DIM ** -0.5)
        rv_b = qkvg[:, 2 * BW:3 * BW].astype(BF16)
        ret.update(rq=rq, rk_b=rk.astype(BF16), rv_b=rv_b, g=qkvg[:, 3 * BW:],
                   o=_dot(rq * rxi_ref[...], rstate[...]))
        qv_same = _qhead_of(r2) == _head_of(c2)
        rstate[...] = rstate[...] * rdec_ref[...] + jnp.where(
            qv_same, _dot_tn(rk * rzeta_ref[...], rv_b), 0.0)

    def ret_heads():
        qs = jnp.concatenate([ret["rq"] * qmask[h] for h in range(HEADS)], axis=0).astype(BF16)
        sc = lax.dot_general(qs, ret["rk_b"], nt_dims, preferred_element_type=F32)
        sc = (sc.reshape(HEADS, ts, ts) * rintra_ref[...]).reshape(HEADS * ts, ts)
        ov = _dot(sc, ret["rv_b"])
        for h in range(HEADS):
            ret["o"] = ret["o"] + hmask[h] * ov[h * ts:(h + 1) * ts]

    def ret_output():
        orr = ret["o"]
        mu = head_sum(orr, 2) * (1.0 / HEAD_DIM)
        cen = orr - mu
        var = head_sum(cen * cen) * (1.0 / HEAD_DIM)
        on = cen * lax.rsqrt(var + NORM_EPS) * rnorm_ref[...]
        merge_branch(3, _silu(ret["g"]) * on)

    P = functools.partial
    nslab = S5_NSTATE // LANES
    pieces = [P(merge_gate_piece, n, c) for n in range(N_BRANCH) for c in range(2)]
    per_slab = len(pieces) // nslab
    slab = [P(s5_slab, j, pieces[j * per_slab:(j + 1) * per_slab]) for j in range(nslab)]
    fill = (slab[0:4] + [lru_inputs] + slab[4:8] + [P(lru_group_step, s) for s in range(3)] + [ret_inputs]
            + [ret_heads, s5_output, lru_finish, ret_output])
    links = ([[P(gdn_inverse_first, n) for n in range(nchunk)]]
             + [[P(gdn_inverse_step, n, k == 4) for n in range(nchunk)] for k in range(5)]
             + [[P(gdn_solve, n) for n in range(nchunk)]]
             + [[P(gdn_state_step, n)] for n in range(nchunk)])
    per_link = -(-len(fill) // len(links))
    s5_inputs()
    gdn_inputs()
    for n in range(nchunk):
        gdn_local(n)
    for link in links:
        for work in link:
            work()
        for work in fill[:per_link]:
            work()
        fill = fill[per_link:]
    for work in fill:
        work()
    gdn_output()
    o_ref[...] = x_ref[...] + jnp.dot(wout_ref[...], mrg_s[...].astype(BF16), preferred_element_type=F32).T


def _retention_tables(ts):
    hh = np.arange(HEADS, dtype=np.float64)
    log_gamma = np.log1p(-np.exp2(-5.0 - hh))
    idx = np.arange(ts, dtype=np.float64)
    rel = idx[:, None] - idx[None, :]
    intra = np.where(rel >= 0, np.exp(np.where(rel >= 0, rel, 0.0)[None] * log_gamma[:, None, None]), 0.0)
    xi = np.exp((idx + 1.0)[None] * log_gamma[:, None])
    zeta = np.exp((ts - 1.0 - idx)[None] * log_gamma[:, None])
    cdec = np.exp(ts * log_gamma)
    qhead = (np.arange(BW) % LANES) // (HEAD_DIM // 2)
    vhead = np.arange(BW) // HEAD_DIM
    xi_q = xi[qhead].T
    zeta_q = zeta[qhead].T
    dec = np.where(qhead[:, None] == vhead[None, :], cdec[qhead][:, None], 0.0)
    f = lambda a: jnp.asarray(a.astype(np.float32))
    return f(intra), f(xi_q), f(zeta_q), f(dec)


def _mixer(x3, cos3, sin3, layer, p, ts):
    b, s, _ = x3.shape
    nsteps = int(math.log2(ts))
    assert 1 << nsteps == ts and s % ts == 0 and ts % GDN_CHUNK == 0
    rintra, rxi, rzeta, rdec = _retention_tables(ts)

    def lspec(shape, single=False):
        nd = len(shape)
        kw = {"pipeline_mode": pl.Buffered(1)} if single else {}
        return pl.BlockSpec((None,) + shape, lambda bi, si: (layer,) + (0,) * nd, **kw)

    def cspec(shape):
        nd = len(shape)
        return pl.BlockSpec(shape, lambda bi, si: (0,) * nd)

    tile = lambda w: pl.BlockSpec((None, ts, w), lambda bi, si: (bi, si, 0))
    in_specs = [
        tile(D_MODEL), tile(LANES), tile(LANES),
        lspec((1, D_MODEL)), lspec((D_MODEL, N_A), single=True),
        lspec((GDN_CONV, 3 * BW)), lspec((1, LANES)), lspec((1, LANES)), lspec((1, BW)),
        lspec((S5_WIDTH, 2 * S5_NSTATE)), lspec((2 * S5_NSTATE, S5_WIDTH)),
        lspec((p["s5_tab"].shape[1], 2 * S5_NSTATE)), lspec((1, BW)), lspec((BW, BW)), lspec((1, BW)),
        lspec((LRU_CONV, BW)), lspec((1, BW)), lspec((BW, 2 * BW)), lspec((1, 2 * BW)), lspec((1, BW)),
        lspec((1, BW)), cspec((HEADS, ts, ts)), cspec((ts, BW)), cspec((ts, BW)), cspec((BW, BW)),
        lspec((N_BRANCH, D_MODEL, D_MODEL), single=True), lspec((N_BRANCH, D_MODEL, BW), single=True),
        lspec((D_MODEL, D_MODEL), single=True),
    ]
    scratch = [
        pltpu.VMEM((HALO, 3 * BW), F32),
        pltpu.VMEM((HALO, BW), F32),
        pltpu.VMEM((ts, 3 * BW), F32),
        pltpu.VMEM((ts, 2 * HEADS * LANES), F32),
        pltpu.VMEM((ts, BW), F32),
        pltpu.VMEM((ts, 2 * S5_NSTATE), F32),
        pltpu.VMEM((2 * S5_NSTATE // LANES, ts, LANES), F32),
        pltpu.VMEM((2 * BW // LANES, ts, LANES), F32),
        pltpu.VMEM((N_BRANCH, D_MODEL, ts), F32),
        pltpu.VMEM((D_MODEL, ts), F32),
        pltpu.VMEM((BW, BW), F32),
        pltpu.VMEM((HALO, 2 * S5_NSTATE), F32),
        pltpu.VMEM((HALO, BW), F32),
        pltpu.VMEM((BW, BW), F32),
    ]
    return pl.pallas_call(
        functools.partial(_mixer_kernel, ts=ts, nsteps=nsteps),
        grid=(b, s // ts),
        in_specs=in_specs,
        out_specs=pl.BlockSpec((None, ts, D_MODEL), lambda bi, si: (bi, si, 0)),
        out_shape=jax.ShapeDtypeStruct((b, s, D_MODEL), F32),
        scratch_shapes=scratch,
        compiler_params=pltpu.CompilerParams(
            dimension_semantics=("arbitrary", "arbitrary"), vmem_limit_bytes=VMEM_LIMIT),
        name="mixer",
    )(x3, cos3, sin3, p["mix_norm"], p["w_in_a"],
      p["gdn_conv_w"], p["gdn_a_log"], p["gdn_dt_bias"], p["gdn_norm_w"],
      p["s5_bbar"], p["s5_cmat"], p["s5_tab"], p["s5_d"], p["s5_w_glu"], p["s5_b_glu"],
      p["lru_conv_w"], p["lru_conv_b"], p["lru_w_ai"], p["lru_b_ai"], p["lru_lambda"],
      p["ret_norm_w"], rintra, rxi, rzeta, rdec,
      p["w_gate"], p["w_branch"], p["w_out"])


def _split_in_proj(w_in):
    offs = np.cumsum([0, 3 * BW, BW, HEADS, HEADS, BW, BW, BW, BW, BW, BW, BW])
    qkv, _, beta, _, s5, _, _, rq, rk, rv, _, gate0 = (int(o) for o in offs)
    nl, d, _ = w_in.shape

    def rope_split(w):
        return w.reshape(nl, d, HEADS, 2, HEAD_DIM // 2).transpose(0, 1, 3, 2, 4).reshape(nl, d, BW)

    ba = jnp.pad(w_in[:, :, beta:s5], ((0, 0), (0, 0), (0, LANES - 2 * HEADS)))
    w_a = jnp.concatenate([w_in[:, :, qkv:beta], ba, w_in[:, :, s5:rq], rope_split(w_in[:, :, rq:rk]),
                           rope_split(w_in[:, :, rk:rv]), w_in[:, :, rv:gate0]], axis=2)
    assert w_a.shape[2] == N_A
    return w_a, w_in[:, :, gate0:]


def _block_diag(blocks):
    nl, n, r, c = blocks.shape
    eye = jnp.eye(n, dtype=blocks.dtype)
    return (blocks[:, :, :, None, :] * eye[None, :, None, :, None]).reshape(nl, n * r, n * c)


def _prepare(prm, ts):
    nl = prm["w_in"].shape[0]
    row = lambda a: a.reshape(nl, 1, -1)
    rep = lambda a, n: jnp.repeat(a, n, axis=-1).reshape(nl, 1, -1)
    decay_lanes = lambda a: jnp.pad(a, ((0, 0), (HEADS, LANES - 2 * HEADS))).reshape(nl, 1, LANES)
    p = {}
    w_a, w_gate = _split_in_proj(prm["w_in"])
    p["w_in_a"] = w_a.astype(BF16)
    d = w_gate.shape[1]
    p["w_gate"] = w_gate.astype(BF16).reshape(nl, d, N_BRANCH, d).transpose(0, 2, 3, 1)
    p["mix_norm"] = row(prm["mix_norm"])
    p["gdn_conv_w"] = prm["gdn_conv_w"]
    p["gdn_a_log"] = decay_lanes(prm["gdn_a_log"])
    p["gdn_dt_bias"] = decay_lanes(prm["gdn_dt_bias"])
    p["gdn_norm_w"] = jnp.tile(prm["gdn_norm_w"], (1, HEADS)).reshape(nl, 1, BW)
    lr = row(prm["s5_lambda_re"])
    li = row(prm["s5_lambda_im"])
    ldt = rep(prm["s5_log_dt"], S5_STATE)
    bre = _block_diag(jnp.swapaxes(prm["s5_b_re"], 2, 3))
    bim = _block_diag(jnp.swapaxes(prm["s5_b_im"], 2, 3))
    cre = _block_diag(jnp.swapaxes(prm["s5_c_re"], 2, 3))
    cim = _block_diag(jnp.swapaxes(prm["s5_c_im"], 2, 3))
    p["s5_bbar"], p["s5_cmat"], p["s5_tab"] = _s5_prep(lr, li, ldt, bre, bim, cre, cim, int(math.log2(ts)))
    p["s5_d"] = row(prm["s5_d"])
    p["s5_w_glu"] = prm["s5_w_glu"].astype(BF16)
    p["s5_b_glu"] = row(prm["s5_b_glu"])
    p["lru_conv_w"] = prm["lru_conv_w"]
    p["lru_conv_b"] = row(prm["lru_conv_b"])
    p["lru_w_ai"] = jnp.concatenate(
        [_block_diag(prm["lru_w_a"]), _block_diag(prm["lru_w_i"])], axis=2).astype(BF16)
    p["lru_b_ai"] = jnp.concatenate([row(prm["lru_b_a"]), row(prm["lru_b_i"])], axis=2)
    p["lru_lambda"] = row(prm["lru_lambda"])
    p["ret_norm_w"] = row(prm["ret_norm_w"])
    p["w_branch"] = jnp.swapaxes(prm["w_branch"].astype(BF16), 2, 3)
    p["w_out"] = jnp.swapaxes(prm["w_out"].astype(BF16), 1, 2)
    for f in ("ffn1", "ffn2"):
        p[f + "_norm"] = row(prm[f + "_norm"])
        for w in ("w_gate", "w_up", "w_down"):
            p[f + "_" + w] = prm[f + "_" + w].astype(BF16)
    return p


def _trunk(x, positions, prm, final_norm, ts):
    b, s, d = x.shape
    nl = prm["w_in"].shape[0]
    p = _prepare(prm, ts)
    cos, sin = _rope_tables(positions.astype(F32).reshape(b * s, 1))
    cos3, sin3 = cos.reshape(b, s, LANES), sin.reshape(b, s, LANES)
    fw = final_norm.reshape(1, d)
    x2 = x.reshape(b * s, d)
    for l in range(nl):
        x2 = _ffn(x2, l, p["ffn1_norm"], p["ffn1_w_gate"], p["ffn1_w_up"], p["ffn1_w_down"], fw, False)
        x2 = _mixer(x2.reshape(b, s, d), cos3, sin3, l, p, ts).reshape(b * s, d)
        x2 = _ffn(x2, l, p["ffn2_norm"], p["ffn2_w_gate"], p["ffn2_w_up"], p["ffn2_w_down"], fw,
                  l == nl - 1)
    return x2.reshape(b, s, d)


def kernel(x, positions, ffn1_norm, ffn1_w_gate, ffn1_w_up, ffn1_w_down, mix_norm, w_in, gdn_conv_w, gdn_a_log, gdn_dt_bias, gdn_norm_w, s5_lambda_re, s5_lambda_im, s5_b_re, s5_b_im, s5_c_re, s5_c_im, s5_d, s5_log_dt, s5_w_glu, s5_b_glu, lru_conv_w, lru_conv_b, lru_w_a, lru_b_a, lru_w_i, lru_b_i, lru_lambda, ret_norm_w, w_branch, w_out, ffn2_norm, ffn2_w_gate, ffn2_w_up, ffn2_w_down, final_norm):
    prm = dict(
        ffn1_norm=ffn1_norm, ffn1_w_gate=ffn1_w_gate, ffn1_w_up=ffn1_w_up, ffn1_w_down=ffn1_w_down,
        mix_norm=mix_norm, w_in=w_in, gdn_conv_w=gdn_conv_w, gdn_a_log=gdn_a_log,
        gdn_dt_bias=gdn_dt_bias, gdn_norm_w=gdn_norm_w, s5_lambda_re=s5_lambda_re,
        s5_lambda_im=s5_lambda_im, s5_b_re=s5_b_re, s5_b_im=s5_b_im, s5_c_re=s5_c_re, s5_c_im=s5_c_im,
        s5_d=s5_d, s5_log_dt=s5_log_dt, s5_w_glu=s5_w_glu, s5_b_glu=s5_b_glu, lru_conv_w=lru_conv_w,
        lru_conv_b=lru_conv_b, lru_w_a=lru_w_a, lru_b_a=lru_b_a, lru_w_i=lru_w_i, lru_b_i=lru_b_i,
        lru_lambda=lru_lambda, ret_norm_w=ret_norm_w, w_branch=w_branch, w_out=w_out,
        ffn2_norm=ffn2_norm, ffn2_w_gate=ffn2_w_gate, ffn2_w_up=ffn2_w_up, ffn2_w_down=ffn2_w_down)
    return _trunk(x, positions, prm, final_norm, SEQ_TILE)
```

```python
import functools
import math

import numpy as np
import jax
import jax.numpy as jnp
from jax import lax
from jax.experimental import pallas as pl
from jax.experimental.pallas import tpu as pltpu

F32 = jnp.float32
BF16 = jnp.bfloat16

D_MODEL = 1024
D_FF = 2816
NORM_EPS = 1e-6
N_BRANCH = 4
HEADS = 4
HEAD_DIM = 64
GDN_CONV = 4
GDN_CHUNK = 64
S5_GROUPS = 16
S5_GROUP_WIDTH = 16
S5_STATE = 64
S5_WIDTH = S5_GROUPS * S5_GROUP_WIDTH
S5_NSTATE = S5_GROUPS * S5_STATE
LRU_WIDTH = 256
LRU_BLOCKS = 4
LRU_CONV = 4
LRU_C = 8.0
ROPE_BASE = 10000.0
BW = 256

LANES = 128
SUB = 8
HALO = SUB

SEQ_TILE = 256
TOK_TILE = 1024
FF_CHUNK = 256
VMEM_LIMIT = 56 * 1024 * 1024

C_QKV = 0
C_Z = C_QKV + 3 * BW
C_BA = C_Z + BW
C_S5 = C_BA + LANES
C_LX = C_S5 + BW
C_LY = C_LX + BW
C_RQ = C_LY + BW
C_RK = C_RQ + BW
C_RV = C_RK + BW
C_RG = C_RV + BW
N_A = C_RG + BW


def _sigmoid(x):
    return 0.5 + 0.5 * jnp.tanh(0.5 * x)


def _silu(x):
    h = 0.5 * x
    return h + h * jnp.tanh(h)


def _gelu_tanh(x):
    return 0.5 * x * (1.0 + jnp.tanh(math.sqrt(2.0 / math.pi) * (x + 0.044715 * (x * x * x))))


def _softplus(x):
    return jnp.maximum(x, 0.0) + jnp.log(1.0 + jnp.exp(-jnp.abs(x)))


def _rms(x, w):
    return x * lax.rsqrt(jnp.mean(x * x, axis=-1, keepdims=True) + NORM_EPS) * w


def _dot(a, b):
    return jnp.dot(a.astype(BF16), b.astype(BF16), preferred_element_type=F32)


def _dot_nt(a, b):
    return lax.dot_general(a.astype(BF16), b.astype(BF16), (((1,), (1,)), ((), ())),
                           preferred_element_type=F32)


def _dot_tn(a, b):
    return lax.dot_general(a.astype(BF16), b.astype(BF16), (((0,), (0,)), ((), ())),
                           preferred_element_type=F32)


def _split_bf16(x, n):
    terms = []
    for _ in range(n - 1):
        t = x.astype(BF16)
        terms.append(t)
        x = x - t.astype(F32)
    terms.append(x.astype(BF16))
    return terms


def _dot_wide_rhs(m_bf16, x, n):
    w = x.shape[1]
    r = jnp.dot(m_bf16, jnp.concatenate(_split_bf16(x, n), axis=1), preferred_element_type=F32)
    return sum(r[:, i * w:(i + 1) * w] for i in range(n))


def _dot_wide_lhs(x, m_bf16, n):
    h = x.shape[0]
    r = jnp.dot(jnp.concatenate(_split_bf16(x, n), axis=0), m_bf16, preferred_element_type=F32)
    return sum(r[i * h:(i + 1) * h] for i in range(n))


def _iota2(shape, axis):
    return lax.broadcasted_iota(jnp.int32, shape, axis)


def _head_of(i):
    return lax.shift_right_logical(i, 6)


def _qhead_of(i):
    return lax.shift_right_logical(lax.bitwise_and(i, LANES - 1), 5)


def _ffn_kernel(x_ref, nw_ref, wg_ref, wu_ref, wd_ref, fw_ref, o_ref, *, final):
    x = x_ref[...]
    hn = _rms(x, nw_ref[...]).astype(BF16)
    acc = jnp.zeros(x.shape, F32)
    for c in range(D_FF // FF_CHUNK):
        cs = slice(c * FF_CHUNK, (c + 1) * FF_CHUNK)
        g = jnp.dot(hn, wg_ref[:, cs], preferred_element_type=F32)
        u = jnp.dot(hn, wu_ref[:, cs], preferred_element_type=F32)
        a = (_silu(g) * u).astype(BF16)
        acc = acc + jnp.dot(a, wd_ref[cs, :], preferred_element_type=F32)
    y = x + 0.5 * acc
    if final:
        y = _rms(y, fw_ref[...])
    o_ref[...] = y


def _ffn(x2, layer, nw, wg, wu, wd, fw, final):
    t = x2.shape[0]
    tm = min(TOK_TILE, t)
    wspec = lambda shape: pl.BlockSpec((None,) + shape, lambda i: (layer, 0, 0),
                                       pipeline_mode=pl.Buffered(1))
    return pl.pallas_call(
        functools.partial(_ffn_kernel, final=final),
        grid=(t // tm,),
        in_specs=[
            pl.BlockSpec((tm, D_MODEL), lambda i: (i, 0)),
            pl.BlockSpec((None, 1, D_MODEL), lambda i: (layer, 0, 0)),
            wspec((D_MODEL, D_FF)),
            wspec((D_MODEL, D_FF)),
            wspec((D_FF, D_MODEL)),
            pl.BlockSpec((1, D_MODEL), lambda i: (0, 0)),
        ],
        out_specs=pl.BlockSpec((tm, D_MODEL), lambda i: (i, 0)),
        out_shape=jax.ShapeDtypeStruct((t, D_MODEL), F32),
        compiler_params=pltpu.CompilerParams(
            dimension_semantics=("arbitrary",), vmem_limit_bytes=VMEM_LIMIT),
        name="ffn",
    )(x2, nw, wg, wu, wd, fw)


def _rope_kernel(pos_ref, freq_ref, cos_ref, sin_ref):
    ang = pos_ref[...] * freq_ref[...]
    cos_ref[...] = jnp.cos(ang)
    sin_ref[...] = jnp.sin(ang)


def _rope_tables(pos_f32):
    t = pos_f32.shape[0]
    rows = min(1024, t)
    half = HEAD_DIM // 2
    freq = np.power(ROPE_BASE, -np.arange(half, dtype=np.float32) / half).astype(np.float32)
    freq = jnp.asarray(np.tile(freq, LANES // half)[None, :])
    return pl.pallas_call(
        _rope_kernel,
        grid=(t // rows,),
        in_specs=[pl.BlockSpec((rows, 1), lambda i: (i, 0)),
                  pl.BlockSpec((1, LANES), lambda i: (0, 0))],
        out_specs=[pl.BlockSpec((rows, LANES), lambda i: (i, 0)),
                   pl.BlockSpec((rows, LANES), lambda i: (i, 0))],
        out_shape=[jax.ShapeDtypeStruct((t, LANES), F32)] * 2,
        name="rope_tables",
    )(pos_f32, freq)


def _s5_re(j):
    return slice(2 * j * LANES, (2 * j + 1) * LANES)


def _s5_im(j):
    return slice((2 * j + 1) * LANES, (2 * j + 2) * LANES)


def _pow_rows(nsteps):
    return SUB * ((nsteps + SUB - 1) // SUB)


def _s5_prep_kernel(lr_ref, li_ref, ldt_ref, bre_ref, bim_ref, cre_ref, cim_ref,
                    bbar_ref, cmat_ref, tab_ref, *, nsteps):
    lr, li = lr_ref[...], li_ref[...]
    dt = jnp.exp(ldt_ref[...])
    mag = jnp.exp(lr * dt)
    ar, ai = mag * jnp.cos(li * dt), mag * jnp.sin(li * dt)
    den = lr * lr + li * li
    nr = ar - 1.0
    cr = (nr * lr + ai * li) / den
    ci = (ai * lr - nr * li) / den
    bre, bim = bre_ref[...], bim_ref[...]
    bbar_re = (cr * bre - ci * bim).astype(BF16)
    bbar_im = (cr * bim + ci * bre).astype(BF16)
    cre = cre_ref[...].astype(BF16)
    cim = (-cim_ref[...]).astype(BF16)
    for j in range(S5_NSTATE // LANES):
        src = slice(j * LANES, (j + 1) * LANES)
        bbar_ref[:, _s5_re(j)] = bbar_re[:, src]
        bbar_ref[:, _s5_im(j)] = bbar_im[:, src]
        cmat_ref[_s5_re(j), :] = cre[src, :]
        cmat_ref[_s5_im(j), :] = cim[src, :]

    def put(r, vr, vi):
        for j in range(S5_NSTATE // LANES):
            src = slice(j * LANES, (j + 1) * LANES)
            tab_ref[r:r + 1, _s5_re(j)] = vr[:, src]
            tab_ref[r:r + 1, _s5_im(j)] = vi[:, src]

    zero = jnp.zeros_like(ar)
    npow = _pow_rows(nsteps)
    pows = []
    pr, pi = ar, ai
    for k in range(npow):
        put(k, pr, pi)
        pows.append((pr, pi))
        pr, pi = pr * pr - pi * pi, 2.0 * pr * pi
    pr, pi = ar, ai
    for r in range(SUB):
        put(npow + r, pr, pi)
        pr, pi = pr * ar - pi * ai, pr * ai + pi * ar
    for k in range(3):
        for r in range(SUB):
            keep = r >= (1 << k)
            put(npow + SUB + k * SUB + r, pows[k][0] if keep else zero, pows[k][1] if keep else zero)


def _s5_prep(lr, li, ldt, bre, bim, cre, cim, nsteps):
    nl = lr.shape[0]
    nrows = _pow_rows(nsteps) + 4 * SUB
    vec = pl.BlockSpec((None, 1, S5_NSTATE), lambda l: (l, 0, 0))
    return pl.pallas_call(
        functools.partial(_s5_prep_kernel, nsteps=nsteps),
        grid=(nl,),
        in_specs=[vec, vec, vec,
                  pl.BlockSpec((None, S5_WIDTH, S5_NSTATE), lambda l: (l, 0, 0)),
                  pl.BlockSpec((None, S5_WIDTH, S5_NSTATE), lambda l: (l, 0, 0)),
                  pl.BlockSpec((None, S5_NSTATE, S5_WIDTH), lambda l: (l, 0, 0)),
                  pl.BlockSpec((None, S5_NSTATE, S5_WIDTH), lambda l: (l, 0, 0))],
        out_specs=[pl.BlockSpec((None, S5_WIDTH, 2 * S5_NSTATE), lambda l: (l, 0, 0)),
                   pl.BlockSpec((None, 2 * S5_NSTATE, S5_WIDTH), lambda l: (l, 0, 0)),
                   pl.BlockSpec((None, nrows, 2 * S5_NSTATE), lambda l: (l, 0, 0))],
        out_shape=[jax.ShapeDtypeStruct((nl, S5_WIDTH, 2 * S5_NSTATE), BF16),
                   jax.ShapeDtypeStruct((nl, 2 * S5_NSTATE, S5_WIDTH), BF16),
                   jax.ShapeDtypeStruct((nl, nrows, 2 * S5_NSTATE), F32)],
        name="s5_prep",
    )(lr, li, ldt, bre, bim, cre, cim)


def _mixer_kernel(
        x_ref, cos_ref, sin_ref, nw_ref, win_ref,
        gconv_ref, galog_ref, gdtb_ref, gnorm_ref,
        bbar_ref, cmat_ref, tab_ref, s5d_ref, wglu_ref, bglu_ref,
        lcw_ref, lcb_ref, lwai_ref, lbai_ref, llam_ref,
        rnorm_ref, rintra_ref, rxi_ref, rzeta_ref, rdec_ref,
        wgate_ref, wbr_ref, wout_ref,
        o_ref,
        gbuf, lbuf, qkv_s, gcb_s, og_s, xs_s, ge_s, lg_s, gate_s, mrg_s, gstate, s5c, lruc, rstate,
        *, ts, nsteps):
    first = pl.program_id(1) == 0

    @pl.when(first)
    def _():
        gbuf[...] = jnp.zeros(gbuf.shape, F32)
        lbuf[...] = jnp.zeros(lbuf.shape, F32)
        gstate[...] = jnp.zeros(gstate.shape, F32)
        rstate[...] = jnp.zeros(rstate.shape, F32)
        s5c[...] = jnp.zeros(s5c.shape, F32)
        lruc[...] = jnp.zeros(lruc.shape, F32)

    hn = _rms(x_ref[...], nw_ref[...]).astype(BF16)

    def proj(c0, width):
        return jnp.dot(hn, win_ref[:, c0:c0 + width], preferred_element_type=F32)

    r2 = _iota2((BW, BW), 0)
    c2 = _iota2((BW, BW), 1)
    same_head = _head_of(r2) == _head_of(c2)
    ones_bd = jnp.where(same_head, 1.0, 0.0).astype(BF16)
    lane = _iota2((1, BW), 1)

    def head_sum(v, terms=1):
        return _dot_wide_lhs(v, ones_bd, terms)

    def causal_conv(cur, hist_ref, w_ref, taps):
        ext = jnp.concatenate([hist_ref[...], cur], axis=0)
        hist_ref[...] = cur[ts - HALO:ts, :]
        y = w_ref[taps - 1:taps, :] * cur
        for d in range(1, taps):
            y = y + w_ref[taps - 1 - d:taps - d, :] * pltpu.roll(ext, d, 0)[HALO:, :]
        return y

    def merge_gate_piece(n, c):
        w = D_MODEL // 2
        gate_s[n, :, c * w:(c + 1) * w] = jnp.tanh(0.5 * jnp.dot(
            hn, wgate_ref[:, n * D_MODEL + c * w:n * D_MODEL + (c + 1) * w], preferred_element_type=F32))

    def merge_branch(n, out_n):
        bp = jnp.dot(out_n.astype(BF16), wbr_ref[n], preferred_element_type=F32)
        term = bp + gate_s[n] * bp
        if merged["started"]:
            mrg_s[...] = mrg_s[...] + term
        else:
            mrg_s[...] = term
            merged["started"] = True

    merged = {"started": False}

    def gdn_inputs():
        qkv = _silu(causal_conv(proj(C_QKV, 3 * BW), gbuf, gconv_ref, GDN_CONV))
        q, k_, v = qkv[:, 0:BW], qkv[:, BW:2 * BW], qkv[:, 2 * BW:3 * BW]
        ss = head_sum(jnp.concatenate([q * q, k_ * k_], axis=0))
        qkv_s[:, 0:BW] = q * lax.rsqrt(ss[0:ts] + 1e-6) * (HEAD_DIM ** -0.5)
        qkv_s[:, BW:2 * BW] = k_ * lax.rsqrt(ss[ts:] + 1e-6)
        qkv_s[:, 2 * BW:3 * BW] = v
        ba = s5["zbu"][:, BW:BW + LANES]
        beta = _sigmoid(ba)
        g = -jnp.exp(galog_ref[...]) * _softplus(ba + gdtb_ref[...])
        rt = _iota2((ts, ts), 0)
        ct = _iota2((ts, ts), 1)
        mchunk = jnp.where((_head_of(rt) == _head_of(ct)) & (ct <= rt), 1.0, 0.0).astype(BF16)
        gc = _dot_wide_rhs(mchunk, g, 3)
        for h in range(HEADS):
            gcb_s[:, h * LANES:(h + 1) * LANES] = jnp.broadcast_to(gc[:, HEADS + h:HEADS + h + 1], (ts, LANES))
            gcb_s[:, (HEADS + h) * LANES:(HEADS + h + 1) * LANES] = jnp.broadcast_to(beta[:, h:h + 1], (ts, LANES))

    hmask = [jnp.where(_head_of(lane) == h, 1.0, 0.0) for h in range(HEADS)]
    bd_b = jnp.where(same_head, 1.0, 0.0).astype(BF16)
    rc = _iota2((GDN_CHUNK, BW), 0)
    jc = lax.bitwise_and(_iota2((GDN_CHUNK, BW), 1), HEAD_DIM - 1)
    incl = jc <= rc
    strict = jc < rc
    eye = jnp.where(jc == rc, 1.0, 0.0)
    low_lane = _iota2((GDN_CHUNK, LANES), 1) < HEAD_DIM

    def stack(m):
        mb = m.astype(BF16)
        return jnp.concatenate([mb] * HEADS, axis=0) * bd_b

    def per_head_lanes(blocks):
        return jnp.concatenate([jnp.where(low_lane, blocks[0], blocks[1]),
                                jnp.where(low_lane, blocks[2], blocks[3])], axis=1)

    nchunk = ts // GDN_CHUNK
    crow = [slice(n * GDN_CHUNK, (n + 1) * GDN_CHUNK) for n in range(nchunk)]
    nt_dims = (((1,), (1,)), ((), ()))
    chunks = [dict() for _ in range(nchunk)]

    def gdn_local(n):
        qc = qkv_s[crow[n], 0:BW]
        kc = qkv_s[crow[n], BW:2 * BW]
        vc = qkv_s[crow[n], 2 * BW:3 * BW]
        gh = [gcb_s[crow[n], h * LANES:(h + 1) * LANES] for h in range(HEADS)]
        bh = [gcb_s[crow[n], (HEADS + h) * LANES:(HEADS + h + 1) * LANES] for h in range(HEADS)]
        g64 = per_head_lanes(gh)
        b64 = per_head_lanes(bh)
        grow = jnp.concatenate(gh, axis=0).T[0:GDN_CHUNK, :]
        decay = jnp.where(incl, jnp.exp(jnp.where(incl, g64 - grow, 0.0)), 0.0)
        ks_b = stack(kc)
        kq = lax.dot_general(jnp.concatenate([kc, qc], axis=0).astype(BF16), ks_b, nt_dims,
                             preferred_element_type=F32)
        kk, qk = kq[0:GDN_CHUNK], kq[GDN_CHUNK:]
        low = jnp.where(strict, b64 * kk * decay, 0.0)
        e64 = jnp.exp(g64)
        glast = g64[GDN_CHUNK - 1:GDN_CHUNK, :]
        chunks[n].update(
            tinv=eye - low, p=low, p_bd=stack(low),
            rhs=jnp.concatenate([stack(vc * b64), stack(kc * (b64 * e64))], axis=1),
            attn_b=(qk * decay).astype(BF16),
            q_dec=(qc * e64).astype(BF16), k_dec=(kc * jnp.exp(glast - g64)).astype(BF16),
            cdec=jnp.exp(glast))

    def gdn_inverse_first(n):
        c = chunks[n]
        c["p"] = jnp.dot(c["p"].astype(BF16), c["p_bd"], preferred_element_type=F32)
        c["p_bd"] = stack(c["p"])

    def gdn_inverse_step(n, last):
        c = chunks[n]
        if last:
            c["tinv"] = c["tinv"] + jnp.dot(c["tinv"].astype(BF16), c["p_bd"], preferred_element_type=F32)
            return
        both = jnp.dot(jnp.concatenate([c["p"], c["tinv"]], axis=0).astype(BF16), c["p_bd"],
                       preferred_element_type=F32)
        c["tinv"] = c["tinv"] + both[GDN_CHUNK:]
        c["p"] = both[0:GDN_CHUNK]
        c["p_bd"] = stack(c["p"])

    def gdn_solve(n):
        c = chunks[n]
        tinv_b = c["tinv"].astype(BF16)
        uw = jnp.dot(tinv_b, c["rhs"], preferred_element_type=F32)
        c["u"] = uw[:, 0:BW]
        c["wq"] = jnp.concatenate([uw[:, BW:].astype(BF16), c["q_dec"]], axis=0)

    def gdn_state_step(n):
        c = chunks[n]
        st = gstate[...]
        st_b = st.astype(BF16)
        ws = jnp.dot(c["wq"], st_b, preferred_element_type=F32)
        v_new = c["u"] - ws[0:GDN_CHUNK]
        kv = lax.dot_general(c["k_dec"], v_new.astype(BF16), (((0,), (0,)), ((), ())),
                             preferred_element_type=F32)
        gstate[...] = st * c["cdec"] + jnp.where(same_head, kv, 0.0)
        og_s[crow[n], :] = ws[GDN_CHUNK:] + jnp.dot(c["attn_b"], stack(v_new), preferred_element_type=F32)

    def gdn_output():
        og = og_s[...]
        og = og * lax.rsqrt(head_sum(og * og) * (1.0 / HEAD_DIM) + NORM_EPS) * gnorm_ref[...]
        merge_branch(0, og * _silu(s5["zbu"][:, 0:BW]))

    ngrp = ts // SUB
    npow = _pow_rows(nsteps)
    rowe = _iota2((ngrp, LANES), 0)
    s5 = {}

    def s5_inputs():
        s5["zbu"] = proj(C_Z, C_LX - C_Z)
        s5["u"] = s5["zbu"][:, C_S5 - C_Z:]
        s5["u_b"] = s5["u"].astype(BF16)

    def s5_slab(j, between=()):
        between = list(between)
        if between:
            between.pop(0)()
        lr_, li_ = _s5_re(j), _s5_im(j)
        if j % 2 == 0:
            s5["bu"] = jnp.dot(s5["u_b"], bbar_ref[:, 2 * j * LANES:(2 * j + 4) * LANES],
                               preferred_element_type=F32)
        bu = s5["bu"][:, (j % 2) * 2 * LANES:(j % 2 + 1) * 2 * LANES]
        xr = bu[:, 0:LANES].reshape(ngrp, SUB, LANES)
        xi = bu[:, LANES:].reshape(ngrp, SUB, LANES)
        for s in range(3):
            m0 = npow + SUB + s * SUB
            mr, mi = tab_ref[m0:m0 + SUB, lr_][None], tab_ref[m0:m0 + SUB, li_][None]
            sr, si = pltpu.roll(xr, 1 << s, 1), pltpu.roll(xi, 1 << s, 1)
            xr, xi = xr + (mr * sr - mi * si), xi + (mr * si + mi * sr)
        if between:
            between.pop(0)()
        ge_s[2 * j] = xr.reshape(ts, LANES)
        ge_s[2 * j + 1] = xi.reshape(ts, LANES)
        er = ge_s[2 * j, pl.ds(SUB - 1, ngrp, stride=SUB), :]
        ei = ge_s[2 * j + 1, pl.ds(SUB - 1, ngrp, stride=SUB), :]
        cr, ci = s5c[0:1, lr_], s5c[0:1, li_]
        a8r, a8i = tab_ref[3:4, lr_], tab_ref[3:4, li_]
        er = er + jnp.where(rowe == 0, a8r * cr - a8i * ci, 0.0)
        ei = ei + jnp.where(rowe == 0, a8r * ci + a8i * cr, 0.0)
        for s in range(3, nsteps):
            d = 1 << (s - 3)
            pr, pi = tab_ref[s:s + 1, lr_], tab_ref[s:s + 1, li_]
            sr = jnp.where(rowe >= d, pltpu.roll(er, d, 0), 0.0)
            si = jnp.where(rowe >= d, pltpu.roll(ei, d, 0), 0.0)
            er, ei = er + (pr * sr - pi * si), ei + (pr * si + pi * sr)
        s5c[0:1, lr_] = er[ngrp - 1:ngrp, :]
        s5c[0:1, li_] = ei[ngrp - 1:ngrp, :]
        pr_ = jnp.where(rowe == 0, cr, pltpu.roll(er, 1, 0))
        pi_ = jnp.where(rowe == 0, ci, pltpu.roll(ei, 1, 0))
        tr, ti = tab_ref[npow:npow + SUB, lr_], tab_ref[npow:npow + SUB, li_]
        if between:
            between.pop(0)()
        for gi in range(ngrp):
            rs = slice(gi * SUB, (gi + 1) * SUB)
            cgr, cgi = pr_[gi:gi + 1, :], pi_[gi:gi + 1, :]
            xs_s[rs, lr_] = xr[gi] + (tr * cgr - ti * cgi)
            xs_s[rs, li_] = xi[gi] + (tr * cgi + ti * cgr)

    def s5_output():
        y5 = (_dot(xs_s[:, 0:S5_NSTATE], cmat_ref[0:S5_NSTATE, :])
              + _dot(xs_s[:, S5_NSTATE:], cmat_ref[S5_NSTATE:, :]) + s5d_ref[...] * s5["u"])
        y5 = _gelu_tanh(y5)
        y5 = y5 * _sigmoid(_dot(y5, wglu_ref[...]) + bglu_ref[...])
        merge_branch(1, y5)

    rsub = lax.broadcasted_iota(jnp.int32, (ngrp, SUB, BW), 1)
    rowg = _iota2((ngrp, BW), 0)
    lru = {}

    def lru_inputs():
        lxy = proj(C_LX, 2 * BW)
        lru["y"] = lxy[:, BW:]
        xc = causal_conv(lxy[:, 0:BW], lbuf, lcw_ref, LRU_CONV) + lcb_ref[...]
        gates = _sigmoid(_dot(xc, lwai_ref[...]) + lbai_ref[...])
        rg, ig = gates[:, 0:BW], gates[:, BW:]
        log_a = -LRU_C * rg * _softplus(-llam_ref[...])
        a = jnp.exp(log_a)
        mult = jnp.sqrt(1.0 - a * a)
        lru["a"] = a.reshape(ngrp, SUB, BW)
        lru["x"] = (xc * ig * mult).reshape(ngrp, SUB, BW)

    def lru_group_step(s):
        d = 1 << s
        sa = jnp.where(rsub >= d, pltpu.roll(lru["a"], d, 1), 1.0)
        sx = jnp.where(rsub >= d, pltpu.roll(lru["x"], d, 1), 0.0)
        lru["x"] = lru["x"] + lru["a"] * sx
        lru["a"] = lru["a"] * sa

    def lru_finish():
        a2, x2 = lru["a"].reshape(ts, BW), lru["x"].reshape(ts, BW)
        halves = BW // LANES
        for c in range(halves):
            lg_s[c] = a2[:, c * LANES:(c + 1) * LANES]
            lg_s[halves + c] = x2[:, c * LANES:(c + 1) * LANES]
        ends = pl.ds(SUB - 1, ngrp, stride=SUB)
        ea = jnp.concatenate([lg_s[c, ends, :] for c in range(halves)], axis=1)
        ex = jnp.concatenate([lg_s[halves + c, ends, :] for c in range(halves)], axis=1)
        carry = lruc[0:1, :]
        ex = ex + jnp.where(rowg == 0, ea * carry, 0.0)
        for s in range(nsteps - 3):
            d = 1 << s
            sa = jnp.where(rowg >= d, pltpu.roll(ea, d, 0), 1.0)
            sx = jnp.where(rowg >= d, pltpu.roll(ex, d, 0), 0.0)
            ex = ex + ea * sx
            ea = ea * sa
        lruc[0:1, :] = ex[ngrp - 1:ngrp, :]
        enter = jnp.where(rowg == 0, carry, pltpu.roll(ex, 1, 0))
        hx = jnp.concatenate(
            [lru["x"][gi] + lru["a"][gi] * enter[gi:gi + 1, :] for gi in range(ngrp)], axis=0)
        merge_branch(2, hx * _gelu_tanh(lru["y"]))

    ret = {}
    qmask = [jnp.where(_qhead_of(lane) == h, 1.0, 0.0) for h in range(HEADS)]

    def ret_inputs():
        cos, sin = cos_ref[...], sin_ref[...]

        def rope(t):
            t1, t2 = t[:, 0:LANES], t[:, LANES:]
            return jnp.concatenate([t1 * cos - t2 * sin, t1 * sin + t2 * cos], axis=1)

        qkvg = proj(C_RQ, 4 * BW)
        rq = rope(qkvg[:, 0:BW])
        rk = rope(qkvg[:, BW:2 * BW]) * (HEAD_DIM ** -0.5)
        rv_b = qkvg[:, 2 * BW:3 * BW].astype(BF16)
        ret.update(rq=rq, rk_b=rk.astype(BF16), rv_b=rv_b, g=qkvg[:, 3 * BW:],
                   o=_dot(rq * rxi_ref[...], rstate[...]))
        qv_same = _qhead_of(r2) == _head_of(c2)
        rstate[...] = rstate[...] * rdec_ref[...] + jnp.where(
            qv_same, _dot_tn(rk * rzeta_ref[...], rv_b), 0.0)

    def ret_heads():
        qs = jnp.concatenate([ret["rq"] * qmask[h] for h in range(HEADS)], axis=0).astype(BF16)
        sc = lax.dot_general(qs, ret["rk_b"], nt_dims, preferred_element_type=F32)
        sc = (sc.reshape(HEADS, ts, ts) * rintra_ref[...]).reshape(HEADS * ts, ts)
        ov = _dot(sc, ret["rv_b"])
        for h in range(HEADS):
            ret["o"] = ret["o"] + hmask[h] * ov[h * ts:(h + 1) * ts]

    def ret_output():
        orr = ret["o"]
        mu = head_sum(orr, 2) * (1.0 / HEAD_DIM)
        cen = orr - mu
        var = head_sum(cen * cen) * (1.0 / HEAD_DIM)
        on = cen * lax.rsqrt(var + NORM_EPS) * rnorm_ref[...]
        merge_branch(3, _silu(ret["g"]) * on)

    P = functools.partial
    nslab = S5_NSTATE // LANES
    pieces = [P(merge_gate_piece, n, c) for n in range(N_BRANCH) for c in range(2)]
    per_slab = len(pieces) // nslab
    slab = [P(s5_slab, j, pieces[j * per_slab:(j + 1) * per_slab]) for j in range(nslab)]
    fill = (slab[0:8] + [P(lru_group_step, s) for s in range(3)]
            + [ret_heads, s5_output, lru_finish, ret_output])
    links = ([[P(gdn_inverse_first, n) for n in range(nchunk)]]
             + [[P(gdn_inverse_step, n, k == 4) for n in range(nchunk)] for k in range(5)]
             + [[P(gdn_solve, n) for n in range(nchunk)]]
             + [[P(gdn_state_step, n)] for n in range(nchunk)])
    per_link = -(-len(fill) // len(links))
    s5_inputs()
    gdn_inputs()
    lru_inputs()
    ret_inputs()
    for n in range(nchunk):
        gdn_local(n)
    for link in links:
        for work in link:
            work()
        for work in fill[:per_link]:
            work()
        fill = fill[per_link:]
    for work in fill:
        work()
    gdn_output()
    o_ref[...] = x_ref[...] + jnp.dot((0.5 * mrg_s[...]).astype(BF16), wout_ref[...],
                                      preferred_element_type=F32)


def _retention_tables(ts):
    hh = np.arange(HEADS, dtype=np.float64)
    log_gamma = np.log1p(-np.exp2(-5.0 - hh))
    idx = np.arange(ts, dtype=np.float64)
    rel = idx[:, None] - idx[None, :]
    intra = np.where(rel >= 0, np.exp(np.where(rel >= 0, rel, 0.0)[None] * log_gamma[:, None, None]), 0.0)
    xi = np.exp((idx + 1.0)[None] * log_gamma[:, None])
    zeta = np.exp((ts - 1.0 - idx)[None] * log_gamma[:, None])
    cdec = np.exp(ts * log_gamma)
    qhead = (np.arange(BW) % LANES) // (HEAD_DIM // 2)
    vhead = np.arange(BW) // HEAD_DIM
    xi_q = xi[qhead].T
    zeta_q = zeta[qhead].T
    dec = np.where(qhead[:, None] == vhead[None, :], cdec[qhead][:, None], 0.0)
    f = lambda a: jnp.asarray(a.astype(np.float32))
    return f(intra), f(xi_q), f(zeta_q), f(dec)


def _mixer(x3, cos3, sin3, layer, p, ts):
    b, s, _ = x3.shape
    nsteps = int(math.log2(ts))
    assert 1 << nsteps == ts and s % ts == 0 and ts % GDN_CHUNK == 0
    rintra, rxi, rzeta, rdec = _retention_tables(ts)

    def lspec(shape, single=False):
        nd = len(shape)
        kw = {"pipeline_mode": pl.Buffered(1)} if single else {}
        return pl.BlockSpec((None,) + shape, lambda bi, si: (layer,) + (0,) * nd, **kw)

    def cspec(shape):
        nd = len(shape)
        return pl.BlockSpec(shape, lambda bi, si: (0,) * nd)

    tile = lambda w: pl.BlockSpec((None, ts, w), lambda bi, si: (bi, si, 0))
    in_specs = [
        tile(D_MODEL), tile(LANES), tile(LANES),
        lspec((1, D_MODEL)), lspec((D_MODEL, N_A), single=True),
        lspec((GDN_CONV, 3 * BW)), lspec((1, LANES)), lspec((1, LANES)), lspec((1, BW)),
        lspec((S5_WIDTH, 2 * S5_NSTATE)), lspec((2 * S5_NSTATE, S5_WIDTH)),
        lspec((p["s5_tab"].shape[1], 2 * S5_NSTATE)), lspec((1, BW)), lspec((BW, BW)), lspec((1, BW)),
        lspec((LRU_CONV, BW)), lspec((1, BW)), lspec((BW, 2 * BW)), lspec((1, 2 * BW)), lspec((1, BW)),
        lspec((1, BW)), cspec((HEADS, ts, ts)), cspec((ts, BW)), cspec((ts, BW)), cspec((BW, BW)),
        lspec((D_MODEL, N_BRANCH * D_MODEL), single=True), lspec((N_BRANCH, BW, D_MODEL), single=True),
        lspec((D_MODEL, D_MODEL), single=True),
    ]
    scratch = [
        pltpu.VMEM((HALO, 3 * BW), F32),
        pltpu.VMEM((HALO, BW), F32),
        pltpu.VMEM((ts, 3 * BW), F32),
        pltpu.VMEM((ts, 2 * HEADS * LANES), F32),
        pltpu.VMEM((ts, BW), F32),
        pltpu.VMEM((ts, 2 * S5_NSTATE), F32),
        pltpu.VMEM((2 * S5_NSTATE // LANES, ts, LANES), F32),
        pltpu.VMEM((2 * BW // LANES, ts, LANES), F32),
        pltpu.VMEM((N_BRANCH, ts, D_MODEL), F32),
        pltpu.VMEM((ts, D_MODEL), F32),
        pltpu.VMEM((BW, BW), F32),
        pltpu.VMEM((HALO, 2 * S5_NSTATE), F32),
        pltpu.VMEM((HALO, BW), F32),
        pltpu.VMEM((BW, BW), F32),
    ]
    return pl.pallas_call(
        functools.partial(_mixer_kernel, ts=ts, nsteps=nsteps),
        grid=(b, s // ts),
        in_specs=in_specs,
        out_specs=pl.BlockSpec((None, ts, D_MODEL), lambda bi, si: (bi, si, 0)),
        out_shape=jax.ShapeDtypeStruct((b, s, D_MODEL), F32),
        scratch_shapes=scratch,
        compiler_params=pltpu.CompilerParams(
            dimension_semantics=("arbitrary", "arbitrary"), vmem_limit_bytes=VMEM_LIMIT),
        name="mixer",
    )(x3, cos3, sin3, p["mix_norm"], p["w_in_a"],
      p["gdn_conv_w"], p["gdn_a_log"], p["gdn_dt_bias"], p["gdn_norm_w"],
      p["s5_bbar"], p["s5_cmat"], p["s5_tab"], p["s5_d"], p["s5_w_glu"], p["s5_b_glu"],
      p["lru_conv_w"], p["lru_conv_b"], p["lru_w_ai"], p["lru_b_ai"], p["lru_lambda"],
      p["ret_norm_w"], rintra, rxi, rzeta, rdec,
      p["w_gate"], p["w_branch"], p["w_out"])


def _split_in_proj(w_in):
    offs = np.cumsum([0, 3 * BW, BW, HEADS, HEADS, BW, BW, BW, BW, BW, BW, BW])
    qkv, _, beta, _, s5, _, _, rq, rk, rv, _, gate0 = (int(o) for o in offs)
    nl, d, _ = w_in.shape

    def rope_split(w):
        return w.reshape(nl, d, HEADS, 2, HEAD_DIM // 2).transpose(0, 1, 3, 2, 4).reshape(nl, d, BW)

    ba = jnp.pad(w_in[:, :, beta:s5], ((0, 0), (0, 0), (0, LANES - 2 * HEADS)))
    w_a = jnp.concatenate([w_in[:, :, qkv:beta], ba, w_in[:, :, s5:rq], rope_split(w_in[:, :, rq:rk]),
                           rope_split(w_in[:, :, rk:rv]), w_in[:, :, rv:gate0]], axis=2)
    assert w_a.shape[2] == N_A
    return w_a, w_in[:, :, gate0:]


def _block_diag(blocks):
    nl, n, r, c = blocks.shape
    eye = jnp.eye(n, dtype=blocks.dtype)
    return (blocks[:, :, :, None, :] * eye[None, :, None, :, None]).reshape(nl, n * r, n * c)


def _prepare(prm, ts):
    nl = prm["w_in"].shape[0]
    row = lambda a: a.reshape(nl, 1, -1)
    rep = lambda a, n: jnp.repeat(a, n, axis=-1).reshape(nl, 1, -1)
    decay_lanes = lambda a: jnp.pad(a, ((0, 0), (HEADS, LANES - 2 * HEADS))).reshape(nl, 1, LANES)
    p = {}
    w_a, w_gate = _split_in_proj(prm["w_in"])
    p["w_in_a"] = w_a.astype(BF16)
    p["w_gate"] = w_gate.astype(BF16)
    p["mix_norm"] = row(prm["mix_norm"])
    p["gdn_conv_w"] = prm["gdn_conv_w"]
    p["gdn_a_log"] = decay_lanes(prm["gdn_a_log"])
    p["gdn_dt_bias"] = decay_lanes(prm["gdn_dt_bias"])
    p["gdn_norm_w"] = jnp.tile(prm["gdn_norm_w"], (1, HEADS)).reshape(nl, 1, BW)
    lr = row(prm["s5_lambda_re"])
    li = row(prm["s5_lambda_im"])
    ldt = rep(prm["s5_log_dt"], S5_STATE)
    bre = _block_diag(jnp.swapaxes(prm["s5_b_re"], 2, 3))
    bim = _block_diag(jnp.swapaxes(prm["s5_b_im"], 2, 3))
    cre = _block_diag(jnp.swapaxes(prm["s5_c_re"], 2, 3))
    cim = _block_diag(jnp.swapaxes(prm["s5_c_im"], 2, 3))
    p["s5_bbar"], p["s5_cmat"], p["s5_tab"] = _s5_prep(lr, li, ldt, bre, bim, cre, cim, int(math.log2(ts)))
    p["s5_d"] = row(prm["s5_d"])
    p["s5_w_glu"] = prm["s5_w_glu"].astype(BF16)
    p["s5_b_glu"] = row(prm["s5_b_glu"])
    p["lru_conv_w"] = prm["lru_conv_w"]
    p["lru_conv_b"] = row(prm["lru_conv_b"])
    p["lru_w_ai"] = jnp.concatenate(
        [_block_diag(prm["lru_w_a"]), _block_diag(prm["lru_w_i"])], axis=2).astype(BF16)
    p["lru_b_ai"] = jnp.concatenate([row(prm["lru_b_a"]), row(prm["lru_b_i"])], axis=2)
    p["lru_lambda"] = row(prm["lru_lambda"])
    p["ret_norm_w"] = row(prm["ret_norm_w"])
    p["w_branch"] = prm["w_branch"].astype(BF16)
    p["w_out"] = prm["w_out"].astype(BF16)
    for f in ("ffn1", "ffn2"):
        p[f + "_norm"] = row(prm[f + "_norm"])
        for w in ("w_gate", "w_up", "w_down"):
            p[f + "_" + w] = prm[f + "_" + w].astype(BF16)
    return p


def _trunk(x, positions, prm, final_norm, ts):
    b, s, d = x.shape
    nl = prm["w_in"].shape[0]
    p = _prepare(prm, ts)
    cos, sin = _rope_tables(positions.astype(F32).reshape(b * s, 1))
    cos3, sin3 = cos.reshape(b, s, LANES), sin.reshape(b, s, LANES)
    fw = final_norm.reshape(1, d)
    x2 = x.reshape(b * s, d)
    for l in range(nl):
        x2 = _ffn(x2, l, p["ffn1_norm"], p["ffn1_w_gate"], p["ffn1_w_up"], p["ffn1_w_down"], fw, False)
        x2 = _mixer(x2.reshape(b, s, d), cos3, sin3, l, p, ts).reshape(b * s, d)
        x2 = _ffn(x2, l, p["ffn2_norm"], p["ffn2_w_gate"], p["ffn2_w_up"], p["ffn2_w_down"], fw,
                  l == nl - 1)
    return x2.reshape(b, s, d)


def kernel(x, positions, ffn1_norm, ffn1_w_gate, ffn1_w_up, ffn1_w_down, mix_norm, w_in, gdn_conv_w, gdn_a_log, gdn_dt_bias, gdn_norm_w, s5_lambda_re, s5_lambda_im, s5_b_re, s5_b_im, s5_c_re, s5_c_im, s5_d, s5_log_dt, s5_w_glu, s5_b_glu, lru_conv_w, lru_conv_b, lru_w_a, lru_b_a, lru_w_i, lru_b_i, lru_lambda, ret_norm_w, w_branch, w_out, ffn2_norm, ffn2_w_gate, ffn2_w_up, ffn2_w_down, final_norm):
    prm = dict(
        ffn1_norm=ffn1_norm, ffn1_w_gate=ffn1_w_gate, ffn1_w_up=ffn1_w_up, ffn1_w_down=ffn1_w_down,
        mix_norm=mix_norm, w_in=w_in, gdn_conv_w=gdn_conv_w, gdn_a_log=gdn_a_log,
        gdn_dt_bias=gdn_dt_bias, gdn_norm_w=gdn_norm_w, s5_lambda_re=s5_lambda_re,
        s5_lambda_im=s5_lambda_im, s5_b_re=s5_b_re, s5_b_im=s5_b_im, s5_c_re=s5_c_re, s5_c_im=s5_c_im,
        s5_d=s5_d, s5_log_dt=s5_log_dt, s5_w_glu=s5_w_glu, s5_b_glu=s5_b_glu, lru_conv_w=lru_conv_w,
        lru_conv_b=lru_conv_b, lru_w_a=lru_w_a, lru_b_a=lru_b_a, lru_w_i=lru_w_i, lru_b_i=lru_b_i,
        lru_lambda=lru_lambda, ret_norm_w=ret_norm_w, w_branch=w_branch, w_out=w_out,
        ffn2_norm=ffn2_norm, ffn2_w_gate=ffn2_w_gate, ffn2_w_up=ffn2_w_up, ffn2_w_down=ffn2_w_down)
    return _trunk(x, positions, prm, final_norm, SEQ_TILE)
```

```python
import functools
import math

import numpy as np
import jax
import jax.numpy as jnp
from jax import lax
from jax.experimental import pallas as pl
from jax.experimental.pallas import tpu as pltpu

F32 = jnp.float32
BF16 = jnp.bfloat16

D_MODEL = 1024
D_FF = 2816
NORM_EPS = 1e-6
N_BRANCH = 4
HEADS = 4
HEAD_DIM = 64
GDN_CONV = 4
GDN_CHUNK = 64
S5_GROUPS = 16
S5_GROUP_WIDTH = 16
S5_STATE = 64
S5_WIDTH = S5_GROUPS * S5_GROUP_WIDTH
S5_NSTATE = S5_GROUPS * S5_STATE
LRU_WIDTH = 256
LRU_BLOCKS = 4
LRU_CONV = 4
LRU_C = 8.0
ROPE_BASE = 10000.0
BW = 256

LANES = 128
SUB = 8
HALO = SUB

SEQ_TILE = 256
TOK_TILE = 1024
FF_CHUNK = 256
VMEM_LIMIT = 56 * 1024 * 1024

C_QKV = 0
C_Z = C_QKV + 3 * BW
C_BA = C_Z + BW
C_S5 = C_BA + LANES
C_LX = C_S5 + BW
C_LY = C_LX + BW
C_RQ = C_LY + BW
C_RK = C_RQ + BW
C_RV = C_RK + BW
C_RG = C_RV + BW
N_A = C_RG + BW


def _sigmoid(x):
    return 0.5 + 0.5 * jnp.tanh(0.5 * x)


def _silu(x):
    h = 0.5 * x
    return h + h * jnp.tanh(h)


def _gelu_tanh(x):
    return 0.5 * x * (1.0 + jnp.tanh(math.sqrt(2.0 / math.pi) * (x + 0.044715 * (x * x * x))))


def _softplus(x):
    return jnp.maximum(x, 0.0) + jnp.log(1.0 + jnp.exp(-jnp.abs(x)))


def _rms(x, w):
    return x * lax.rsqrt(jnp.mean(x * x, axis=-1, keepdims=True) + NORM_EPS) * w


def _dot(a, b):
    return jnp.dot(a.astype(BF16), b.astype(BF16), preferred_element_type=F32)


def _dot_nt(a, b):
    return lax.dot_general(a.astype(BF16), b.astype(BF16), (((1,), (1,)), ((), ())),
                           preferred_element_type=F32)


def _dot_tn(a, b):
    return lax.dot_general(a.astype(BF16), b.astype(BF16), (((0,), (0,)), ((), ())),
                           preferred_element_type=F32)


def _split_bf16(x, n):
    terms = []
    for _ in range(n - 1):
        t = x.astype(BF16)
        terms.append(t)
        x = x - t.astype(F32)
    terms.append(x.astype(BF16))
    return terms


def _dot_wide_rhs(m_bf16, x, n):
    w = x.shape[1]
    r = jnp.dot(m_bf16, jnp.concatenate(_split_bf16(x, n), axis=1), preferred_element_type=F32)
    return sum(r[:, i * w:(i + 1) * w] for i in range(n))


def _dot_wide_lhs(x, m_bf16, n):
    h = x.shape[0]
    r = jnp.dot(jnp.concatenate(_split_bf16(x, n), axis=0), m_bf16, preferred_element_type=F32)
    return sum(r[i * h:(i + 1) * h] for i in range(n))


def _iota2(shape, axis):
    return lax.broadcasted_iota(jnp.int32, shape, axis)


def _head_of(i):
    return lax.shift_right_logical(i, 6)


def _qhead_of(i):
    return lax.shift_right_logical(lax.bitwise_and(i, LANES - 1), 5)


def _ffn_kernel(x_ref, nw_ref, wg_ref, wu_ref, wd_ref, fw_ref, o_ref, *, final):
    x = x_ref[...]
    hn = _rms(x, nw_ref[...]).astype(BF16)
    acc = jnp.zeros(x.shape, F32)
    for c in range(D_FF // FF_CHUNK):
        cs = slice(c * FF_CHUNK, (c + 1) * FF_CHUNK)
        g = jnp.dot(hn, wg_ref[:, cs], preferred_element_type=F32)
        u = jnp.dot(hn, wu_ref[:, cs], preferred_element_type=F32)
        a = (_silu(g) * u).astype(BF16)
        acc = acc + jnp.dot(a, wd_ref[cs, :], preferred_element_type=F32)
    y = x + 0.5 * acc
    if final:
        y = _rms(y, fw_ref[...])
    o_ref[...] = y


def _ffn(x2, layer, nw, wg, wu, wd, fw, final):
    t = x2.shape[0]
    tm = min(TOK_TILE, t)
    wspec = lambda shape: pl.BlockSpec((None,) + shape, lambda i: (layer, 0, 0),
                                       pipeline_mode=pl.Buffered(1))
    return pl.pallas_call(
        functools.partial(_ffn_kernel, final=final),
        grid=(t // tm,),
        in_specs=[
            pl.BlockSpec((tm, D_MODEL), lambda i: (i, 0)),
            pl.BlockSpec((None, 1, D_MODEL), lambda i: (layer, 0, 0)),
            wspec((D_MODEL, D_FF)),
            wspec((D_MODEL, D_FF)),
            wspec((D_FF, D_MODEL)),
            pl.BlockSpec((1, D_MODEL), lambda i: (0, 0)),
        ],
        out_specs=pl.BlockSpec((tm, D_MODEL), lambda i: (i, 0)),
        out_shape=jax.ShapeDtypeStruct((t, D_MODEL), F32),
        compiler_params=pltpu.CompilerParams(
            dimension_semantics=("arbitrary",), vmem_limit_bytes=VMEM_LIMIT),
        name="ffn",
    )(x2, nw, wg, wu, wd, fw)


def _rope_kernel(pos_ref, freq_ref, cos_ref, sin_ref):
    ang = pos_ref[...] * freq_ref[...]
    cos_ref[...] = jnp.cos(ang)
    sin_ref[...] = jnp.sin(ang)


def _rope_tables(pos_f32):
    t = pos_f32.shape[0]
    rows = min(1024, t)
    half = HEAD_DIM // 2
    freq = np.power(ROPE_BASE, -np.arange(half, dtype=np.float32) / half).astype(np.float32)
    freq = jnp.asarray(np.tile(freq, LANES // half)[None, :])
    return pl.pallas_call(
        _rope_kernel,
        grid=(t // rows,),
        in_specs=[pl.BlockSpec((rows, 1), lambda i: (i, 0)),
                  pl.BlockSpec((1, LANES), lambda i: (0, 0))],
        out_specs=[pl.BlockSpec((rows, LANES), lambda i: (i, 0)),
                   pl.BlockSpec((rows, LANES), lambda i: (i, 0))],
        out_shape=[jax.ShapeDtypeStruct((t, LANES), F32)] * 2,
        name="rope_tables",
    )(pos_f32, freq)


def _s5_re(j):
    return slice(2 * j * LANES, (2 * j + 1) * LANES)


def _s5_im(j):
    return slice((2 * j + 1) * LANES, (2 * j + 2) * LANES)


def _pow_rows(nsteps):
    return SUB * ((nsteps + SUB - 1) // SUB)


def _s5_prep_kernel(lr_ref, li_ref, ldt_ref, bre_ref, bim_ref, cre_ref, cim_ref,
                    bbar_ref, cmat_ref, tab_ref, *, nsteps):
    lr, li = lr_ref[...], li_ref[...]
    dt = jnp.exp(ldt_ref[...])
    mag = jnp.exp(lr * dt)
    ar, ai = mag * jnp.cos(li * dt), mag * jnp.sin(li * dt)
    den = lr * lr + li * li
    nr = ar - 1.0
    cr = (nr * lr + ai * li) / den
    ci = (ai * lr - nr * li) / den
    bre, bim = bre_ref[...], bim_ref[...]
    bbar_re = (cr * bre - ci * bim).astype(BF16)
    bbar_im = (cr * bim + ci * bre).astype(BF16)
    cre = cre_ref[...].astype(BF16)
    cim = (-cim_ref[...]).astype(BF16)
    for j in range(S5_NSTATE // LANES):
        src = slice(j * LANES, (j + 1) * LANES)
        bbar_ref[:, _s5_re(j)] = bbar_re[:, src]
        bbar_ref[:, _s5_im(j)] = bbar_im[:, src]
        cmat_ref[_s5_re(j), :] = cre[src, :]
        cmat_ref[_s5_im(j), :] = cim[src, :]

    def put(r, vr, vi):
        for j in range(S5_NSTATE // LANES):
            src = slice(j * LANES, (j + 1) * LANES)
            tab_ref[r:r + 1, _s5_re(j)] = vr[:, src]
            tab_ref[r:r + 1, _s5_im(j)] = vi[:, src]

    zero = jnp.zeros_like(ar)
    npow = _pow_rows(nsteps)
    pows = []
    pr, pi = ar, ai
    for k in range(npow):
        put(k, pr, pi)
        pows.append((pr, pi))
        pr, pi = pr * pr - pi * pi, 2.0 * pr * pi
    pr, pi = ar, ai
    for r in range(SUB):
        put(npow + r, pr, pi)
        pr, pi = pr * ar - pi * ai, pr * ai + pi * ar
    for k in range(3):
        for r in range(SUB):
            keep = r >= (1 << k)
            put(npow + SUB + k * SUB + r, pows[k][0] if keep else zero, pows[k][1] if keep else zero)


def _s5_prep(lr, li, ldt, bre, bim, cre, cim, nsteps):
    nl = lr.shape[0]
    nrows = _pow_rows(nsteps) + 4 * SUB
    vec = pl.BlockSpec((None, 1, S5_NSTATE), lambda l: (l, 0, 0))
    return pl.pallas_call(
        functools.partial(_s5_prep_kernel, nsteps=nsteps),
        grid=(nl,),
        in_specs=[vec, vec, vec,
                  pl.BlockSpec((None, S5_WIDTH, S5_NSTATE), lambda l: (l, 0, 0)),
                  pl.BlockSpec((None, S5_WIDTH, S5_NSTATE), lambda l: (l, 0, 0)),
                  pl.BlockSpec((None, S5_NSTATE, S5_WIDTH), lambda l: (l, 0, 0)),
                  pl.BlockSpec((None, S5_NSTATE, S5_WIDTH), lambda l: (l, 0, 0))],
        out_specs=[pl.BlockSpec((None, S5_WIDTH, 2 * S5_NSTATE), lambda l: (l, 0, 0)),
                   pl.BlockSpec((None, 2 * S5_NSTATE, S5_WIDTH), lambda l: (l, 0, 0)),
                   pl.BlockSpec((None, nrows, 2 * S5_NSTATE), lambda l: (l, 0, 0))],
        out_shape=[jax.ShapeDtypeStruct((nl, S5_WIDTH, 2 * S5_NSTATE), BF16),
                   jax.ShapeDtypeStruct((nl, 2 * S5_NSTATE, S5_WIDTH), BF16),
                   jax.ShapeDtypeStruct((nl, nrows, 2 * S5_NSTATE), F32)],
        name="s5_prep",
    )(lr, li, ldt, bre, bim, cre, cim)


def _mixer_kernel(
        x_ref, cos_ref, sin_ref, nw_ref, win_ref,
        gconv_ref, galog_ref, gdtb_ref, gnorm_ref,
        bbar_ref, cmat_ref, tab_ref, s5d_ref, wglu_ref, bglu_ref,
        lcw_ref, lcb_ref, lwai_ref, lbai_ref, llam_ref,
        rnorm_ref, rintra_ref, rxi_ref, rzeta_ref, rdec_ref,
        wgate_ref, wbr_ref, wout_ref,
        o_ref,
        gbuf, lbuf, qkv_s, gcb_s, og_s, xs_s, ge_s, lg_s, gate_s, mrg_s, gstate, s5c, lruc, rstate,
        *, ts, nsteps):
    first = pl.program_id(1) == 0

    @pl.when(first)
    def _():
        gbuf[...] = jnp.zeros(gbuf.shape, F32)
        lbuf[...] = jnp.zeros(lbuf.shape, F32)
        gstate[...] = jnp.zeros(gstate.shape, F32)
        rstate[...] = jnp.zeros(rstate.shape, F32)
        s5c[...] = jnp.zeros(s5c.shape, F32)
        lruc[...] = jnp.zeros(lruc.shape, F32)

    hn = _rms(x_ref[...], nw_ref[...]).astype(BF16)

    def proj(c0, width):
        return jnp.dot(hn, win_ref[:, c0:c0 + width], preferred_element_type=F32)

    r2 = _iota2((BW, BW), 0)
    c2 = _iota2((BW, BW), 1)
    same_head = _head_of(r2) == _head_of(c2)
    ones_bd = jnp.where(same_head, 1.0, 0.0).astype(BF16)
    lane = _iota2((1, BW), 1)

    def head_sum(v, terms=1):
        return _dot_wide_lhs(v, ones_bd, terms)

    def causal_conv(cur, hist_ref, w_ref, taps):
        ext = jnp.concatenate([hist_ref[...], cur], axis=0)
        hist_ref[...] = cur[ts - HALO:ts, :]
        y = w_ref[taps - 1:taps, :] * cur
        for d in range(1, taps):
            y = y + w_ref[taps - 1 - d:taps - d, :] * pltpu.roll(ext, d, 0)[HALO:, :]
        return y

    def merge_gate_piece(n, c):
        w = D_MODEL // 2
        gate_s[n, :, c * w:(c + 1) * w] = jnp.tanh(0.5 * jnp.dot(
            hn, wgate_ref[:, n * D_MODEL + c * w:n * D_MODEL + (c + 1) * w], preferred_element_type=F32))

    def merge_branch(n, out_n):
        bp = jnp.dot(out_n.astype(BF16), wbr_ref[n], preferred_element_type=F32)
        term = bp + gate_s[n] * bp
        if merged["started"]:
            mrg_s[...] = mrg_s[...] + term
        else:
            mrg_s[...] = term
            merged["started"] = True

    merged = {"started": False}

    def gdn_inputs():
        qkv = _silu(causal_conv(proj(C_QKV, 3 * BW), gbuf, gconv_ref, GDN_CONV))
        q, k_, v = qkv[:, 0:BW], qkv[:, BW:2 * BW], qkv[:, 2 * BW:3 * BW]
        ss = head_sum(jnp.concatenate([q * q, k_ * k_], axis=0))
        qkv_s[:, 0:BW] = q * lax.rsqrt(ss[0:ts] + 1e-6) * (HEAD_DIM ** -0.5)
        qkv_s[:, BW:2 * BW] = k_ * lax.rsqrt(ss[ts:] + 1e-6)
        qkv_s[:, 2 * BW:3 * BW] = v
        ba = s5["zbu"][:, BW:BW + LANES]
        beta = _sigmoid(ba)
        g = -jnp.exp(galog_ref[...]) * _softplus(ba + gdtb_ref[...])
        rt = _iota2((ts, ts), 0)
        ct = _iota2((ts, ts), 1)
        mchunk = jnp.where((_head_of(rt) == _head_of(ct)) & (ct <= rt), 1.0, 0.0).astype(BF16)
        gc = _dot_wide_rhs(mchunk, g, 3)
        for h in range(HEADS):
            gcb_s[:, h * LANES:(h + 1) * LANES] = jnp.broadcast_to(gc[:, HEADS + h:HEADS + h + 1], (ts, LANES))
            gcb_s[:, (HEADS + h) * LANES:(HEADS + h + 1) * LANES] = jnp.broadcast_to(beta[:, h:h + 1], (ts, LANES))

    hmask = [jnp.where(_head_of(lane) == h, 1.0, 0.0) for h in range(HEADS)]
    bd_b = jnp.where(same_head, 1.0, 0.0).astype(BF16)
    rc = _iota2((GDN_CHUNK, BW), 0)
    jc = lax.bitwise_and(_iota2((GDN_CHUNK, BW), 1), HEAD_DIM - 1)
    incl = jc <= rc
    strict = jc < rc
    eye = jnp.where(jc == rc, 1.0, 0.0)
    low_lane = _iota2((GDN_CHUNK, LANES), 1) < HEAD_DIM

    def stack(m):
        mb = m.astype(BF16)
        return jnp.concatenate([mb] * HEADS, axis=0) * bd_b

    def per_head_lanes(blocks):
        return jnp.concatenate([jnp.where(low_lane, blocks[0], blocks[1]),
                                jnp.where(low_lane, blocks[2], blocks[3])], axis=1)

    nchunk = ts // GDN_CHUNK
    crow = [slice(n * GDN_CHUNK, (n + 1) * GDN_CHUNK) for n in range(nchunk)]
    nt_dims = (((1,), (1,)), ((), ()))
    chunks = [dict() for _ in range(nchunk)]

    def gdn_local(n):
        qc = qkv_s[crow[n], 0:BW]
        kc = qkv_s[crow[n], BW:2 * BW]
        vc = qkv_s[crow[n], 2 * BW:3 * BW]
        gh = [gcb_s[crow[n], h * LANES:(h + 1) * LANES] for h in range(HEADS)]
        bh = [gcb_s[crow[n], (HEADS + h) * LANES:(HEADS + h + 1) * LANES] for h in range(HEADS)]
        g64 = per_head_lanes(gh)
        b64 = per_head_lanes(bh)
        grow = jnp.concatenate(gh, axis=0).T[0:GDN_CHUNK, :]
        decay = jnp.where(incl, jnp.exp(jnp.where(incl, g64 - grow, 0.0)), 0.0)
        ks_b = stack(kc)
        kq = lax.dot_general(jnp.concatenate([kc, qc], axis=0).astype(BF16), ks_b, nt_dims,
                             preferred_element_type=F32)
        kk, qk = kq[0:GDN_CHUNK], kq[GDN_CHUNK:]
        low = jnp.where(strict, b64 * kk * decay, 0.0)
        e64 = jnp.exp(g64)
        glast = g64[GDN_CHUNK - 1:GDN_CHUNK, :]
        chunks[n].update(
            tinv=eye - low, p=low, p_bd=stack(low),
            rhs=jnp.concatenate([stack(vc * b64), stack(kc * (b64 * e64))], axis=1),
            attn_b=(qk * decay).astype(BF16),
            q_dec=(qc * e64).astype(BF16), k_dec=(kc * jnp.exp(glast - g64)).astype(BF16),
            cdec=jnp.exp(glast))

    def gdn_inverse_first(n):
        c = chunks[n]
        c["p"] = jnp.dot(c["p"].astype(BF16), c["p_bd"], preferred_element_type=F32)
        c["p_bd"] = stack(c["p"])

    def gdn_inverse_step(n, last):
        c = chunks[n]
        if last:
            c["tinv"] = c["tinv"] + jnp.dot(c["tinv"].astype(BF16), c["p_bd"], preferred_element_type=F32)
            return
        both = jnp.dot(jnp.concatenate([c["p"], c["tinv"]], axis=0).astype(BF16), c["p_bd"],
                       preferred_element_type=F32)
        c["tinv"] = c["tinv"] + both[GDN_CHUNK:]
        c["p"] = both[0:GDN_CHUNK]
        c["p_bd"] = stack(c["p"])

    def gdn_solve(n):
        c = chunks[n]
        tinv_b = c["tinv"].astype(BF16)
        uw = jnp.dot(tinv_b, c["rhs"], preferred_element_type=F32)
        c["u"] = uw[:, 0:BW]
        c["wq"] = jnp.concatenate([uw[:, BW:].astype(BF16), c["q_dec"]], axis=0)

    def gdn_state_step(n):
        c = chunks[n]
        st = gstate[...]
        st_b = st.astype(BF16)
        ws = jnp.dot(c["wq"], st_b, preferred_element_type=F32)
        v_new = c["u"] - ws[0:GDN_CHUNK]
        kv = lax.dot_general(c["k_dec"], v_new.astype(BF16), (((0,), (0,)), ((), ())),
                             preferred_element_type=F32)
        gstate[...] = st * c["cdec"] + jnp.where(same_head, kv, 0.0)
        og_s[crow[n], :] = ws[GDN_CHUNK:] + jnp.dot(c["attn_b"], stack(v_new), preferred_element_type=F32)

    def gdn_output():
        og = og_s[...]
        og = og * lax.rsqrt(head_sum(og * og) * (1.0 / HEAD_DIM) + NORM_EPS) * gnorm_ref[...]
        merge_branch(0, og * _silu(s5["zbu"][:, 0:BW]))

    ngrp = ts // SUB
    npow = _pow_rows(nsteps)
    rowe = _iota2((ngrp, LANES), 0)
    s5 = {}

    def s5_inputs():
        s5["zbu"] = proj(C_Z, C_LX - C_Z)
        s5["u"] = s5["zbu"][:, C_S5 - C_Z:]
        s5["u_b"] = s5["u"].astype(BF16)

    def s5_slab(j, between=()):
        between = list(between)
        if between:
            between.pop(0)()
        lr_, li_ = _s5_re(j), _s5_im(j)
        if j % 2 == 0:
            s5["bu"] = jnp.dot(s5["u_b"], bbar_ref[:, 2 * j * LANES:(2 * j + 4) * LANES],
                               preferred_element_type=F32)
        bu = s5["bu"][:, (j % 2) * 2 * LANES:(j % 2 + 1) * 2 * LANES]
        xr = bu[:, 0:LANES].reshape(ngrp, SUB, LANES)
        xi = bu[:, LANES:].reshape(ngrp, SUB, LANES)
        for s in range(3):
            m0 = npow + SUB + s * SUB
            mr, mi = tab_ref[m0:m0 + SUB, lr_][None], tab_ref[m0:m0 + SUB, li_][None]
            sr, si = pltpu.roll(xr, 1 << s, 1), pltpu.roll(xi, 1 << s, 1)
            xr, xi = xr + (mr * sr - mi * si), xi + (mr * si + mi * sr)
        if between:
            between.pop(0)()
        ge_s[2 * j] = xr.reshape(ts, LANES)
        ge_s[2 * j + 1] = xi.reshape(ts, LANES)
        er = ge_s[2 * j, pl.ds(SUB - 1, ngrp, stride=SUB), :]
        ei = ge_s[2 * j + 1, pl.ds(SUB - 1, ngrp, stride=SUB), :]
        cr, ci = s5c[0:1, lr_], s5c[0:1, li_]
        a8r, a8i = tab_ref[3:4, lr_], tab_ref[3:4, li_]
        er = er + jnp.where(rowe == 0, a8r * cr - a8i * ci, 0.0)
        ei = ei + jnp.where(rowe == 0, a8r * ci + a8i * cr, 0.0)
        for s in range(3, nsteps):
            d = 1 << (s - 3)
            pr, pi = tab_ref[s:s + 1, lr_], tab_ref[s:s + 1, li_]
            sr = jnp.where(rowe >= d, pltpu.roll(er, d, 0), 0.0)
            si = jnp.where(rowe >= d, pltpu.roll(ei, d, 0), 0.0)
            er, ei = er + (pr * sr - pi * si), ei + (pr * si + pi * sr)
        s5c[0:1, lr_] = er[ngrp - 1:ngrp, :]
        s5c[0:1, li_] = ei[ngrp - 1:ngrp, :]
        pr_ = jnp.where(rowe == 0, cr, pltpu.roll(er, 1, 0))
        pi_ = jnp.where(rowe == 0, ci, pltpu.roll(ei, 1, 0))
        tr, ti = tab_ref[npow:npow + SUB, lr_], tab_ref[npow:npow + SUB, li_]
        if between:
            between.pop(0)()
        for gi in range(ngrp):
            rs = slice(gi * SUB, (gi + 1) * SUB)
            cgr, cgi = pr_[gi:gi + 1, :], pi_[gi:gi + 1, :]
            xs_s[rs, lr_] = xr[gi] + (tr * cgr - ti * cgi)
            xs_s[rs, li_] = xi[gi] + (tr * cgi + ti * cgr)

    def s5_output():
        y5 = (_dot(xs_s[:, 0:S5_NSTATE], cmat_ref[0:S5_NSTATE, :])
              + _dot(xs_s[:, S5_NSTATE:], cmat_ref[S5_NSTATE:, :]) + s5d_ref[...] * s5["u"])
        y5 = _gelu_tanh(y5)
        y5 = y5 * _sigmoid(_dot(y5, wglu_ref[...]) + bglu_ref[...])
        merge_branch(1, y5)

    rsub = lax.broadcasted_iota(jnp.int32, (ngrp, SUB, BW), 1)
    rowg = _iota2((ngrp, BW), 0)
    lru = {}

    def lru_inputs():
        lxy = proj(C_LX, 2 * BW)
        lru["y"] = lxy[:, BW:]
        xc = causal_conv(lxy[:, 0:BW], lbuf, lcw_ref, LRU_CONV) + lcb_ref[...]
        gates = _sigmoid(_dot(xc, lwai_ref[...]) + lbai_ref[...])
        rg, ig = gates[:, 0:BW], gates[:, BW:]
        log_a = -LRU_C * rg * _softplus(-llam_ref[...])
        a = jnp.exp(log_a)
        mult = jnp.sqrt(1.0 - a * a)
        lru["a"] = a.reshape(ngrp, SUB, BW)
        lru["x"] = (xc * ig * mult).reshape(ngrp, SUB, BW)

    def lru_group_step(s):
        d = 1 << s
        sa = jnp.where(rsub >= d, pltpu.roll(lru["a"], d, 1), 1.0)
        sx = jnp.where(rsub >= d, pltpu.roll(lru["x"], d, 1), 0.0)
        lru["x"] = lru["x"] + lru["a"] * sx
        lru["a"] = lru["a"] * sa

    def lru_finish():
        a2, x2 = lru["a"].reshape(ts, BW), lru["x"].reshape(ts, BW)
        halves = BW // LANES
        for c in range(halves):
            lg_s[c] = a2[:, c * LANES:(c + 1) * LANES]
            lg_s[halves + c] = x2[:, c * LANES:(c + 1) * LANES]
        ends = pl.ds(SUB - 1, ngrp, stride=SUB)
        ea = jnp.concatenate([lg_s[c, ends, :] for c in range(halves)], axis=1)
        ex = jnp.concatenate([lg_s[halves + c, ends, :] for c in range(halves)], axis=1)
        carry = lruc[0:1, :]
        ex = ex + jnp.where(rowg == 0, ea * carry, 0.0)
        for s in range(nsteps - 3):
            d = 1 << s
            sa = jnp.where(rowg >= d, pltpu.roll(ea, d, 0), 1.0)
            sx = jnp.where(rowg >= d, pltpu.roll(ex, d, 0), 0.0)
            ex = ex + ea * sx
            ea = ea * sa
        lruc[0:1, :] = ex[ngrp - 1:ngrp, :]
        enter = jnp.where(rowg == 0, carry, pltpu.roll(ex, 1, 0))
        hx = jnp.concatenate(
            [lru["x"][gi] + lru["a"][gi] * enter[gi:gi + 1, :] for gi in range(ngrp)], axis=0)
        merge_branch(2, hx * _gelu_tanh(lru["y"]))

    ret = {}
    qmask = [jnp.where(_qhead_of(lane) == h, 1.0, 0.0) for h in range(HEADS)]

    def ret_inputs():
        cos, sin = cos_ref[...], sin_ref[...]

        def rope(t):
            t1, t2 = t[:, 0:LANES], t[:, LANES:]
            return jnp.concatenate([t1 * cos - t2 * sin, t1 * sin + t2 * cos], axis=1)

        qkvg = proj(C_RQ, 4 * BW)
        rq = rope(qkvg[:, 0:BW])
        rk = rope(qkvg[:, BW:2 * BW]) * (HEAD_DIM ** -0.5)
        rv_b = qkvg[:, 2 * BW:3 * BW].astype(BF16)
        ret.update(rq=rq, rk_b=rk.astype(BF16), rv_b=rv_b, g=qkvg[:, 3 * BW:],
                   o=_dot(rq * rxi_ref[...], rstate[...]))
        qv_same = _qhead_of(r2) == _head_of(c2)
        rstate[...] = rstate[...] * rdec_ref[...] + jnp.where(
            qv_same, _dot_tn(rk * rzeta_ref[...], rv_b), 0.0)

    def ret_heads():
        qs = jnp.concatenate([ret["rq"] * qmask[h] for h in range(HEADS)], axis=0).astype(BF16)
        sc = lax.dot_general(qs, ret["rk_b"], nt_dims, preferred_element_type=F32)
        sc = (sc.reshape(HEADS, ts, ts) * rintra_ref[...]).reshape(HEADS * ts, ts)
        ov = _dot(sc, ret["rv_b"])
        for h in range(HEADS):
            ret["o"] = ret["o"] + hmask[h] * ov[h * ts:(h + 1) * ts]

    def ret_output():
        orr = ret["o"]
        mu = head_sum(orr, 2) * (1.0 / HEAD_DIM)
        cen = orr - mu
        var = head_sum(cen * cen) * (1.0 / HEAD_DIM)
        on = cen * lax.rsqrt(var + NORM_EPS) * rnorm_ref[...]
        merge_branch(3, _silu(ret["g"]) * on)

    P = functools.partial
    nslab = S5_NSTATE // LANES
    pieces = [P(merge_gate_piece, n, c) for n in range(N_BRANCH) for c in range(2)]
    per_slab = len(pieces) // nslab
    slab = [P(s5_slab, j, pieces[j * per_slab:(j + 1) * per_slab]) for j in range(nslab)]
    lsteps = [P(lru_group_step, s) for s in range(3)]
    fill = ([slab[0], lsteps[0], slab[1], lsteps[1], slab[2], lsteps[2], slab[3], ret_heads]
            + slab[4:8] + [ret_output, lru_finish, s5_output])
    links = ([[P(gdn_inverse_first, n) for n in range(nchunk)]]
             + [[P(gdn_inverse_step, n, k == 4) for n in range(nchunk)] for k in range(5)]
             + [[P(gdn_solve, n) for n in range(nchunk)]]
             + [[P(gdn_state_step, n)] for n in range(nchunk)])
    per_link = -(-len(fill) // len(links))
    s5_inputs()
    gdn_inputs()
    lru_inputs()
    ret_inputs()
    for n in range(nchunk):
        gdn_local(n)
    for link in links:
        for work in link:
            work()
        for work in fill[:per_link]:
            work()
        fill = fill[per_link:]
    for work in fill:
        work()
    gdn_output()
    o_ref[...] = x_ref[...] + jnp.dot((0.5 * mrg_s[...]).astype(BF16), wout_ref[...],
                                      preferred_element_type=F32)


def _retention_tables(ts):
    hh = np.arange(HEADS, dtype=np.float64)
    log_gamma = np.log1p(-np.exp2(-5.0 - hh))
    idx = np.arange(ts, dtype=np.float64)
    rel = idx[:, None] - idx[None, :]
    intra = np.where(rel >= 0, np.exp(np.where(rel >= 0, rel, 0.0)[None] * log_gamma[:, None, None]), 0.0)
    xi = np.exp((idx + 1.0)[None] * log_gamma[:, None])
    zeta = np.exp((ts - 1.0 - idx)[None] * log_gamma[:, None])
    cdec = np.exp(ts * log_gamma)
    qhead = (np.arange(BW) % LANES) // (HEAD_DIM // 2)
    vhead = np.arange(BW) // HEAD_DIM
    xi_q = xi[qhead].T
    zeta_q = zeta[qhead].T
    dec = np.where(qhead[:, None] == vhead[None, :], cdec[qhead][:, None], 0.0)
    f = lambda a: jnp.asarray(a.astype(np.float32))
    return f(intra), f(xi_q), f(zeta_q), f(dec)


def _mixer(x3, cos3, sin3, layer, p, ts):
    b, s, _ = x3.shape
    nsteps = int(math.log2(ts))
    assert 1 << nsteps == ts and s % ts == 0 and ts % GDN_CHUNK == 0
    rintra, rxi, rzeta, rdec = _retention_tables(ts)

    def lspec(shape, single=False):
        nd = len(shape)
        kw = {"pipeline_mode": pl.Buffered(1)} if single else {}
        return pl.BlockSpec((None,) + shape, lambda bi, si: (layer,) + (0,) * nd, **kw)

    def cspec(shape):
        nd = len(shape)
        return pl.BlockSpec(shape, lambda bi, si: (0,) * nd)

    tile = lambda w: pl.BlockSpec((None, ts, w), lambda bi, si: (bi, si, 0))
    in_specs = [
        tile(D_MODEL), tile(LANES), tile(LANES),
        lspec((1, D_MODEL)), lspec((D_MODEL, N_A), single=True),
        lspec((GDN_CONV, 3 * BW)), lspec((1, LANES)), lspec((1, LANES)), lspec((1, BW)),
        lspec((S5_WIDTH, 2 * S5_NSTATE)), lspec((2 * S5_NSTATE, S5_WIDTH)),
        lspec((p["s5_tab"].shape[1], 2 * S5_NSTATE)), lspec((1, BW)), lspec((BW, BW)), lspec((1, BW)),
        lspec((LRU_CONV, BW)), lspec((1, BW)), lspec((BW, 2 * BW)), lspec((1, 2 * BW)), lspec((1, BW)),
        lspec((1, BW)), cspec((HEADS, ts, ts)), cspec((ts, BW)), cspec((ts, BW)), cspec((BW, BW)),
        lspec((D_MODEL, N_BRANCH * D_MODEL), single=True), lspec((N_BRANCH, BW, D_MODEL), single=True),
        lspec((D_MODEL, D_MODEL), single=True),
    ]
    scratch = [
        pltpu.VMEM((HALO, 3 * BW), F32),
        pltpu.VMEM((HALO, BW), F32),
        pltpu.VMEM((ts, 3 * BW), F32),
        pltpu.VMEM((ts, 2 * HEADS * LANES), F32),
        pltpu.VMEM((ts, BW), F32),
        pltpu.VMEM((ts, 2 * S5_NSTATE), F32),
        pltpu.VMEM((2 * S5_NSTATE // LANES, ts, LANES), F32),
        pltpu.VMEM((2 * BW // LANES, ts, LANES), F32),
        pltpu.VMEM((N_BRANCH, ts, D_MODEL), F32),
        pltpu.VMEM((ts, D_MODEL), F32),
        pltpu.VMEM((BW, BW), F32),
        pltpu.VMEM((HALO, 2 * S5_NSTATE), F32),
        pltpu.VMEM((HALO, BW), F32),
        pltpu.VMEM((BW, BW), F32),
    ]
    return pl.pallas_call(
        functools.partial(_mixer_kernel, ts=ts, nsteps=nsteps),
        grid=(b, s // ts),
        in_specs=in_specs,
        out_specs=pl.BlockSpec((None, ts, D_MODEL), lambda bi, si: (bi, si, 0)),
        out_shape=jax.ShapeDtypeStruct((b, s, D_MODEL), F32),
        scratch_shapes=scratch,
        compiler_params=pltpu.CompilerParams(
            dimension_semantics=("arbitrary", "arbitrary"), vmem_limit_bytes=VMEM_LIMIT),
        name="mixer",
    )(x3, cos3, sin3, p["mix_norm"], p["w_in_a"],
      p["gdn_conv_w"], p["gdn_a_log"], p["gdn_dt_bias"], p["gdn_norm_w"],
      p["s5_bbar"], p["s5_cmat"], p["s5_tab"], p["s5_d"], p["s5_w_glu"], p["s5_b_glu"],
      p["lru_conv_w"], p["lru_conv_b"], p["lru_w_ai"], p["lru_b_ai"], p["lru_lambda"],
      p["ret_norm_w"], rintra, rxi, rzeta, rdec,
      p["w_gate"], p["w_branch"], p["w_out"])


def _split_in_proj(w_in):
    offs = np.cumsum([0, 3 * BW, BW, HEADS, HEADS, BW, BW, BW, BW, BW, BW, BW])
    qkv, _, beta, _, s5, _, _, rq, rk, rv, _, gate0 = (int(o) for o in offs)
    nl, d, _ = w_in.shape

    def rope_split(w):
        return w.reshape(nl, d, HEADS, 2, HEAD_DIM // 2).transpose(0, 1, 3, 2, 4).reshape(nl, d, BW)

    ba = jnp.pad(w_in[:, :, beta:s5], ((0, 0), (0, 0), (0, LANES - 2 * HEADS)))
    w_a = jnp.concatenate([w_in[:, :, qkv:beta], ba, w_in[:, :, s5:rq], rope_split(w_in[:, :, rq:rk]),
                           rope_split(w_in[:, :, rk:rv]), w_in[:, :, rv:gate0]], axis=2)
    assert w_a.shape[2] == N_A
    return w_a, w_in[:, :, gate0:]


def _block_diag(blocks):
    nl, n, r, c = blocks.shape
    eye = jnp.eye(n, dtype=blocks.dtype)
    return (blocks[:, :, :, None, :] * eye[None, :, None, :, None]).reshape(nl, n * r, n * c)


def _prepare(prm, ts):
    nl = prm["w_in"].shape[0]
    row = lambda a: a.reshape(nl, 1, -1)
    rep = lambda a, n: jnp.repeat(a, n, axis=-1).reshape(nl, 1, -1)
    decay_lanes = lambda a: jnp.pad(a, ((0, 0), (HEADS, LANES - 2 * HEADS))).reshape(nl, 1, LANES)
    p = {}
    w_a, w_gate = _split_in_proj(prm["w_in"])
    p["w_in_a"] = w_a.astype(BF16)
    p["w_gate"] = w_gate.astype(BF16)
    p["mix_norm"] = row(prm["mix_norm"])
    p["gdn_conv_w"] = prm["gdn_conv_w"]
    p["gdn_a_log"] = decay_lanes(prm["gdn_a_log"])
    p["gdn_dt_bias"] = decay_lanes(prm["gdn_dt_bias"])
    p["gdn_norm_w"] = jnp.tile(prm["gdn_norm_w"], (1, HEADS)).reshape(nl, 1, BW)
    lr = row(prm["s5_lambda_re"])
    li = row(prm["s5_lambda_im"])
    ldt = rep(prm["s5_log_dt"], S5_STATE)
    bre = _block_diag(jnp.swapaxes(prm["s5_b_re"], 2, 3))
    bim = _block_diag(jnp.swapaxes(prm["s5_b_im"], 2, 3))
    cre = _block_diag(jnp.swapaxes(prm["s5_c_re"], 2, 3))
    cim = _block_diag(jnp.swapaxes(prm["s5_c_im"], 2, 3))
    p["s5_bbar"], p["s5_cmat"], p["s5_tab"] = _s5_prep(lr, li, ldt, bre, bim, cre, cim, int(math.log2(ts)))
    p["s5_d"] = row(prm["s5_d"])
    p["s5_w_glu"] = prm["s5_w_glu"].astype(BF16)
    p["s5_b_glu"] = row(prm["s5_b_glu"])
    p["lru_conv_w"] = prm["lru_conv_w"]
    p["lru_conv_b"] = row(prm["lru_conv_b"])
    p["lru_w_ai"] = jnp.concatenate(
        [_block_diag(prm["lru_w_a"]), _block_diag(prm["lru_w_i"])], axis=2).astype(BF16)
    p["lru_b_ai"] = jnp.concatenate([row(prm["lru_b_a"]), row(prm["lru_b_i"])], axis=2)
    p["lru_lambda"] = row(prm["lru_lambda"])
    p["ret_norm_w"] = row(prm["ret_norm_w"])
    p["w_branch"] = prm["w_branch"].astype(BF16)
    p["w_out"] = prm["w_out"].astype(BF16)
    for f in ("ffn1", "ffn2"):
        p[f + "_norm"] = row(prm[f + "_norm"])
        for w in ("w_gate", "w_up", "w_down"):
            p[f + "_" + w] = prm[f + "_" + w].astype(BF16)
    return p


def _trunk(x, positions, prm, final_norm, ts):
    b, s, d = x.shape
    nl = prm["w_in"].shape[0]
    p = _prepare(prm, ts)
    cos, sin = _rope_tables(positions.astype(F32).reshape(b * s, 1))
    cos3, sin3 = cos.reshape(b, s, LANES), sin.reshape(b, s, LANES)
    fw = final_norm.reshape(1, d)
    x2 = x.reshape(b * s, d)
    for l in range(nl):
        x2 = _ffn(x2, l, p["ffn1_norm"], p["ffn1_w_gate"], p["ffn1_w_up"], p["ffn1_w_down"], fw, False)
        x2 = _mixer(x2.reshape(b, s, d), cos3, sin3, l, p, ts).reshape(b * s, d)
        x2 = _ffn(x2, l, p["ffn2_norm"], p["ffn2_w_gate"], p["ffn2_w_up"], p["ffn2_w_down"], fw,
                  l == nl - 1)
    return x2.reshape(b, s, d)


def kernel(x, positions, ffn1_norm, ffn1_w_gate, ffn1_w_up, ffn1_w_down, mix_norm, w_in, gdn_conv_w, gdn_a_log, gdn_dt_bias, gdn_norm_w, s5_lambda_re, s5_lambda_im, s5_b_re, s5_b_im, s5_c_re, s5_c_im, s5_d, s5_log_dt, s5_w_glu, s5_b_glu, lru_conv_w, lru_conv_b, lru_w_a, lru_b_a, lru_w_i, lru_b_i, lru_lambda, ret_norm_w, w_branch, w_out, ffn2_norm, ffn2_w_gate, ffn2_w_up, ffn2_w_down, final_norm):
    prm = dict(
        ffn1_norm=ffn1_norm, ffn1_w_gate=ffn1_w_gate, ffn1_w_up=ffn1_w_up, ffn1_w_down=ffn1_w_down,
        mix_norm=mix_norm, w_in=w_in, gdn_conv_w=gdn_conv_w, gdn_a_log=gdn_a_log,
        gdn_dt_bias=gdn_dt_bias, gdn_norm_w=gdn_norm_w, s5_lambda_re=s5_lambda_re,
        s5_lambda_im=s5_lambda_im, s5_b_re=s5_b_re, s5_b_im=s5_b_im, s5_c_re=s5_c_re, s5_c_im=s5_c_im,
        s5_d=s5_d, s5_log_dt=s5_log_dt, s5_w_glu=s5_w_glu, s5_b_glu=s5_b_glu, lru_conv_w=lru_conv_w,
        lru_conv_b=lru_conv_b, lru_w_a=lru_w_a, lru_b_a=lru_b_a, lru_w_i=lru_w_i, lru_b_i=lru_b_i,
        lru_lambda=lru_lambda, ret_norm_w=ret_norm_w, w_branch=w_branch, w_out=w_out,
        ffn2_norm=ffn2_norm, ffn2_w_gate=ffn2_w_gate, ffn2_w_up=ffn2_w_up, ffn2_w_down=ffn2_w_down)
    return _trunk(x, positions, prm, final_norm, SEQ_TILE)
```
